```python
import math
import jax, jax.numpy as jnp
from jax import lax
import numpy as np

D_MODEL = 1024
BATCH = 8
SEQ = 4096
DEPTH = 1

CONV_WIDTH = 512
CONV_K = 3
N_HEADS = 8
HEAD_DIM = 64
V_DIM = 2 * HEAD_DIM
ATTN_QK = N_HEADS * 2 * HEAD_DIM
ATTN_V = N_HEADS * V_DIM
Q_BLOCK = 128
N_BUCKETS = 32
MAX_DISTANCE = 128
N_EXPERTS = 32
TOP_K = 4
D_EXPERT = D_MODEL
SWIGLU_ALPHA = 1.702
SWIGLU_LIMIT = 7.0
MOE_BLOCK = 256
D_IN = 3 * CONV_WIDTH + 2 * ATTN_QK + ATTN_V + 2 * D_MODEL
EPS = 1e-6

kernel_name = "hybrid_conv_diffattn_moe_adaln_block"


def rmsnorm(x, g):
    xf = x.astype(jnp.float32)
    r = xf * lax.rsqrt(jnp.mean(xf * xf, axis=-1, keepdims=True) + EPS)
    return (r * g.astype(jnp.float32)).astype(x.dtype)


def t5_bucket(dist):
    max_exact = N_BUCKETS // 2
    d = jnp.maximum(dist, 1).astype(jnp.float32)
    large = max_exact + (jnp.log(d / max_exact) / math.log(MAX_DISTANCE / max_exact)
                         * (N_BUCKETS - max_exact)).astype(jnp.int32)
    large = jnp.minimum(large, N_BUCKETS - 1)
    return jnp.where(dist < max_exact, dist, large)


def causal_dwconv(u, w, b):
    S = u.shape[1]
    up = jnp.pad(u, ((0, 0), (CONV_K - 1, 0), (0, 0)))
    y = b
    for j in range(CONV_K):
        y = y + up[:, j:j + S] * w[j]
    return y


def diff_attention(q, k, v, lam, rel_table):
    B, S = q.shape[0], q.shape[1]
    nb = S // Q_BLOCK
    scale = HEAD_DIM ** -0.5
    k_pos = jnp.arange(S, dtype=jnp.int32)
    neg = jnp.finfo(jnp.float32).min
    qb = q.reshape(B, nb, Q_BLOCK, N_HEADS, 2, HEAD_DIM).transpose(1, 0, 2, 3, 4, 5)

    def block(args):
        q_blk, i = args
        q_pos = i * Q_BLOCK + jnp.arange(Q_BLOCK, dtype=jnp.int32)
        rel = q_pos[:, None] - k_pos[None, :]
        bias = rel_table[t5_bucket(jnp.maximum(rel, 0))]
        bias = bias.transpose(2, 0, 1).astype(jnp.float32)
        logits = jnp.einsum('bqhmd,bkhmd->bmhqk', q_blk, k,
                            preferred_element_type=jnp.float32) * scale + bias
        logits = jnp.where(rel >= 0, logits, neg)
        probs = jax.nn.softmax(logits, axis=-1)
        attn = probs[:, 0] - lam * probs[:, 1]
        return jnp.einsum('bhqk,bkhe->bqhe', attn.astype(v.dtype), v)

    out = lax.map(block, (qb, jnp.arange(nb, dtype=jnp.int32)))
    return out.transpose(1, 0, 2, 3, 4).reshape(B, S, N_HEADS, V_DIM)


def token_mixer(h, w_in, conv_w, conv_b, w_out_conv, lam, lam_init, subln_g, w_o_attn, rel_table, w_merge):
    B, S, _ = h.shape
    proj = jnp.einsum('bsd,de->bse', h, w_in)
    cuts = [CONV_WIDTH, 2 * CONV_WIDTH, 3 * CONV_WIDTH,
            3 * CONV_WIDTH + ATTN_QK, 3 * CONV_WIDTH + 2 * ATTN_QK,
            3 * CONV_WIDTH + 2 * ATTN_QK + ATTN_V,
            3 * CONV_WIDTH + 2 * ATTN_QK + ATTN_V + D_MODEL]
    cb, cc, cu, q, k, v, ga, gb = jnp.split(proj, cuts, axis=-1)
    y_a = jnp.einsum('bsc,cd->bsd', cb * causal_dwconv(cc * cu, conv_w, conv_b), w_out_conv)
    q = q.reshape(B, S, N_HEADS, 2, HEAD_DIM)
    k = k.reshape(B, S, N_HEADS, 2, HEAD_DIM)
    v = v.reshape(B, S, N_HEADS, V_DIM)
    o = diff_attention(q, k, v, lam, rel_table)
    o = rmsnorm(o, subln_g) * (1.0 - lam_init)
    y_b = jnp.einsum('bse,ed->bsd', o.reshape(B, S, ATTN_V), w_o_attn)
    merged = jax.nn.sigmoid(ga) * y_a + jax.nn.sigmoid(gb) * y_b
    return jnp.einsum('bsd,de->bse', merged, w_merge)


def moe(h, w_router, b_router, w_gate_up, b_gate_up, w_down, b_down):
    B, S, D = h.shape
    n = B * S
    hf = h.reshape(n, D)
    logits = jnp.dot(hf, w_router, preferred_element_type=jnp.float32) + b_router.astype(jnp.float32)
    top_logits, top_idx = lax.top_k(logits, TOP_K)
    top_w = jax.nn.softmax(top_logits, axis=-1)
    a = n * TOP_K
    flat_e = top_idx.reshape(a).astype(jnp.int32)
    flat_t = jnp.repeat(jnp.arange(n, dtype=jnp.int32), TOP_K)
    flat_w = top_w.reshape(a)
    order = jnp.argsort(flat_e)
    se, st, sw = flat_e[order], flat_t[order], flat_w[order]
    counts = jnp.bincount(flat_e, length=N_EXPERTS).astype(jnp.int32)
    padded = (counts + MOE_BLOCK - 1) // MOE_BLOCK * MOE_BLOCK
    start = jnp.cumsum(counts) - counts
    pend = jnp.cumsum(padded)
    pstart = pend - padded
    dest = pstart[se] + jnp.arange(a, dtype=jnp.int32) - start[se]
    n_blocks = -(-a // MOE_BLOCK) + N_EXPERTS
    rows = n_blocks * MOE_BLOCK
    row_tok = jnp.zeros((rows,), jnp.int32).at[dest].set(st)
    row_w = jnp.zeros((rows,), jnp.float32).at[dest].set(sw)
    block_e = jnp.minimum(jnp.searchsorted(pend, jnp.arange(n_blocks, dtype=jnp.int32) * MOE_BLOCK,
                                           side='right'), N_EXPERTS - 1).astype(jnp.int32)

    def expert_block(args):
        tok, wts, e = args
        xb = hf[tok]
        gu = jnp.dot(xb, w_gate_up[e]) + b_gate_up[e]
        glu = jnp.minimum(gu[:, ::2], SWIGLU_LIMIT)
        lin = jnp.clip(gu[:, 1::2], -SWIGLU_LIMIT, SWIGLU_LIMIT)
        act = glu * jax.nn.sigmoid(SWIGLU_ALPHA * glu) * (lin + 1.0)
        y = jnp.dot(act, w_down[e]) + b_down[e]
        return y.astype(jnp.float32) * wts[:, None]

    yb = lax.map(expert_block, (row_tok.reshape(n_blocks, MOE_BLOCK),
                                row_w.reshape(n_blocks, MOE_BLOCK), block_e))
    out = jnp.zeros((n, D), jnp.float32).at[row_tok].add(yb.reshape(rows, D))
    return out.astype(h.dtype).reshape(B, S, D)


def setup_inputs(seed: int = 0) -> dict:
    key = jax.random.key(seed)
    ks = jax.random.split(key, 32)
    f32 = jnp.float32
    L = DEPTH

    def nrm(k, shape, scale):
        return jax.random.normal(k, shape, f32) * scale

    return {
        'x': nrm(ks[0], (BATCH, SEQ, D_MODEL), 1.0),
        'c': nrm(ks[1], (BATCH, D_MODEL), 1.0),
        'w_ada': nrm(ks[2], (L, D_MODEL, 6 * D_MODEL), 0.5 * D_MODEL ** -0.5),
        'b_ada': nrm(ks[3], (L, 6 * D_MODEL), 0.02),
        'norm1_g': 1.0 + nrm(ks[4], (L, D_MODEL), 0.02),
        'norm2_g': 1.0 + nrm(ks[5], (L, D_MODEL), 0.02),
        'w_in': nrm(ks[6], (L, D_MODEL, D_IN), D_MODEL ** -0.5),
        'conv_w': nrm(ks[7], (L, CONV_K, CONV_WIDTH), CONV_K ** -0.5),
        'conv_b': nrm(ks[8], (L, CONV_WIDTH), 0.02),
        'w_out_conv': nrm(ks[9], (L, CONV_WIDTH, D_MODEL), CONV_WIDTH ** -0.5),
        'lambda_q1': nrm(ks[10], (L, HEAD_DIM), 0.1),
        'lambda_k1': nrm(ks[11], (L, HEAD_DIM), 0.1),
        'lambda_q2': nrm(ks[12], (L, HEAD_DIM), 0.1),
        'lambda_k2': nrm(ks[13], (L, HEAD_DIM), 0.1),
        'subln_g': 1.0 + nrm(ks[14], (L, V_DIM), 0.02),
        'w_o_attn': nrm(ks[15], (L, ATTN_V, D_MODEL), ATTN_V ** -0.5),
        'w_merge': nrm(ks[16], (L, D_MODEL, D_MODEL), D_MODEL ** -0.5),
        'w_router': nrm(ks[17], (L, D_MODEL, N_EXPERTS), D_MODEL ** -0.5),
        'b_router': nrm(ks[18], (L, N_EXPERTS), 0.01),
        'w_gate_up': nrm(ks[19], (L, N_EXPERTS, D_MODEL, 2 * D_EXPERT), D_MODEL ** -0.5),
        'b_gate_up': nrm(ks[20], (L, N_EXPERTS, 2 * D_EXPERT), 0.02),
        'w_down': nrm(ks[21], (L, N_EXPERTS, D_EXPERT, D_MODEL), D_EXPERT ** -0.5),
        'b_down': nrm(ks[22], (L, N_EXPERTS, D_MODEL), 0.02),
        'rel_bias_table': nrm(ks[23], (N_BUCKETS, N_HEADS), 0.5),
        'normf_g': 1.0 + nrm(ks[24], (D_MODEL,), 0.02),
    }


def reference(x, c, w_ada, b_ada, norm1_g, norm2_g, w_in, conv_w, conv_b, w_out_conv,
              lambda_q1, lambda_k1, lambda_q2, lambda_k2, subln_g, w_o_attn, w_merge,
              w_router, b_router, w_gate_up, b_gate_up, w_down, b_down, rel_bias_table, normf_g):
    for l in range(DEPTH):
        mod = jnp.dot(jax.nn.silu(c), w_ada[l]) + b_ada[l]
        sh1, sc1, g1, sh2, sc2, g2 = jnp.split(mod[:, None, :], 6, axis=-1)
        lam_init = 0.8 - 0.6 * math.exp(-0.3 * l)
        lam = (jnp.exp(jnp.sum(lambda_q1[l].astype(jnp.float32) * lambda_k1[l].astype(jnp.float32)))
               - jnp.exp(jnp.sum(lambda_q2[l].astype(jnp.float32) * lambda_k2[l].astype(jnp.float32)))
               + lam_init)
        h = rmsnorm(x, norm1_g[l]) * (1.0 + sc1) + sh1
        x = x + g1 * token_mixer(h, w_in[l], conv_w[l], conv_b[l], w_out_conv[l], lam, lam_init,
                                 subln_g[l], w_o_attn[l], rel_bias_table, w_merge[l])
        h = rmsnorm(x, norm2_g[l]) * (1.0 + sc2) + sh2
        x = x + g2 * moe(h, w_router[l], b_router[l], w_gate_up[l], b_gate_up[l], w_down[l], b_down[l])
    return rmsnorm(x, normf_g)
```

```python
import functools
import math

import jax
import jax.numpy as jnp
from jax import lax
from jax.experimental import pallas as pl
from jax.experimental.pallas import tpu as pltpu

F32 = jnp.float32
BF16 = jnp.bfloat16

D_MODEL = 1024
CONV_WIDTH = 512
CONV_K = 3
N_HEADS = 8
HEAD_DIM = 64
V_DIM = 2 * HEAD_DIM
N_BUCKETS = 32
MAX_DISTANCE = 128
N_EXPERTS = 32
TOP_K = 4
SWIGLU_ALPHA = 1.702
SWIGLU_LIMIT = 7.0
EPS = 1e-6
LAM_INIT = 0.8 - 0.6 * math.exp(-0.3 * 0)

D_IN = 3 * CONV_WIDTH + 5 * D_MODEL
COL_Q, COL_K, COL_V, COL_GA, COL_GB = 0, 1024, 2048, 3072, 4096
COL_CB, COL_CC, COL_CU = 5120, 5632, 6144

TM_IN = 1024
TN_IN = 1664
TQ = 512
TM_POST = 512
T_SCAN = 512
T_ROW = 256
ROW_BLOCK = 512
MASK_NEG = -1e30
VMEM_LIMIT = 56 * 1024 * 1024


def _sigmoid(x):
    return 1.0 / (1.0 + jnp.exp(-x))


def _rms(x, g):
    return x * lax.rsqrt(jnp.mean(x * x, axis=-1, keepdims=True) + EPS) * g


def _ada_kernel(c_ref, w_ref, b_ref, lq1_ref, lk1_ref, lq2_ref, lk2_ref, mod_ref, lam_ref):
    c = c_ref[...]
    s = c * _sigmoid(c)
    mod_ref[...] = jnp.dot(s, w_ref[...], precision=lax.Precision.HIGHEST,
                           preferred_element_type=F32) + b_ref[...]
    a1 = jnp.sum(lq1_ref[...] * lk1_ref[...], axis=-1, keepdims=True)
    a2 = jnp.sum(lq2_ref[...] * lk2_ref[...], axis=-1, keepdims=True)
    lam_ref[...] = jnp.broadcast_to(jnp.exp(a1) - jnp.exp(a2) + LAM_INIT, lam_ref.shape)


def _ada(c, w_ada, b_ada, lq1, lk1, lq2, lk2):
    bsz = c.shape[0]
    n_chunks = w_ada.shape[1] // D_MODEL
    vec = pl.BlockSpec((1, HEAD_DIM), lambda j: (0, 0))
    return pl.pallas_call(
        _ada_kernel,
        grid=(n_chunks,),
        in_specs=[pl.BlockSpec((bsz, D_MODEL), lambda j: (0, 0)),
                  pl.BlockSpec((D_MODEL, D_MODEL), lambda j: (0, j)),
                  pl.BlockSpec((1, D_MODEL), lambda j: (0, j)),
                  vec, vec, vec, vec],
        out_specs=[pl.BlockSpec((bsz, D_MODEL), lambda j: (0, j)),
                   pl.BlockSpec((1, 128), lambda j: (0, 0))],
        out_shape=[jax.ShapeDtypeStruct((bsz, n_chunks * D_MODEL), F32),
                   jax.ShapeDtypeStruct((1, 128), F32)],
        name="ada",
    )(c, w_ada, b_ada.reshape(1, -1), lq1.reshape(1, -1), lk1.reshape(1, -1),
      lq2.reshape(1, -1), lk2.reshape(1, -1))


def _inproj_kernel(x_ref, mod_ref, g_ref, w_ref, o_ref, h_scr):
    @pl.when(pl.program_id(2) == 0)
    def _():
        h = _rms(x_ref[0], g_ref[...]) * (1.0 + mod_ref[0, 1:2, :]) + mod_ref[0, 0:1, :]
        h_scr[...] = h.astype(BF16)

    o_ref[0] = jnp.dot(h_scr[...], w_ref[...], preferred_element_type=F32).astype(BF16)


def _inproj(x, mod3, norm1_g, w_in_b):
    bsz, seq, _ = x.shape
    tm = min(TM_IN, seq)
    return pl.pallas_call(
        _inproj_kernel,
        grid=(bsz, seq // tm, D_IN // TN_IN),
        in_specs=[pl.BlockSpec((1, tm, D_MODEL), lambda b, i, j: (b, i, 0)),
                  pl.BlockSpec((1, 6, D_MODEL), lambda b, i, j: (b, 0, 0)),
                  pl.BlockSpec((1, D_MODEL), lambda b, i, j: (0, 0)),
                  pl.BlockSpec((D_MODEL, TN_IN), lambda b, i, j: (0, j))],
        out_specs=pl.BlockSpec((1, tm, TN_IN), lambda b, i, j: (b, i, j)),
        out_shape=jax.ShapeDtypeStruct((bsz, seq, D_IN), BF16),
        scratch_shapes=[pltpu.VMEM((tm, D_MODEL), BF16)],
        compiler_params=pltpu.CompilerParams(
            dimension_semantics=("arbitrary", "arbitrary", "arbitrary"),
            vmem_limit_bytes=VMEM_LIMIT),
        name="inproj",
    )(x, mod3, norm1_g.reshape(1, -1), w_in_b)


def _attn_kernel(lam_ref, q_ref, k_ref, v_ref, bias_ref, g_ref, o_ref, m_scr, l_scr, acc_scr, *, tq):
    i = pl.program_id(2)
    q = q_ref[0] * jnp.asarray(HEAD_DIM ** -0.5, BF16)
    lane = lax.broadcasted_iota(jnp.int32, q.shape, 1)
    zero = jnp.zeros_like(q)
    q_maps = (jnp.where(lane < HEAD_DIM, q, zero), jnp.where(lane >= HEAD_DIM, q, zero))

    m_scr[...] = jnp.full(m_scr.shape, MASK_NEG, F32)
    l_scr[...] = jnp.zeros(l_scr.shape, F32)
    acc_scr[...] = jnp.zeros(acc_scr.shape, F32)

    def step(j, bias):
        k = k_ref[0, pl.ds(pl.multiple_of(j * tq, tq), tq), :]
        v = v_ref[0, pl.ds(pl.multiple_of(j * tq, tq), tq), :]
        for m in range(2):
            s = lax.dot_general(q_maps[m], k, (((1,), (1,)), ((), ())),
                                preferred_element_type=F32)
            if bias is not None:
                s = s + bias
            m_prev = m_scr[m]
            m_new = jnp.maximum(m_prev, jnp.max(s, axis=-1, keepdims=True))
            alpha = jnp.exp(m_prev - m_new)
            p = jnp.exp(s - m_new)
            l_scr[m] = alpha * l_scr[m] + jnp.sum(p, axis=-1, keepdims=True)
            acc_scr[m] = alpha * acc_scr[m] + jnp.dot(p.astype(BF16), v, preferred_element_type=F32)
            m_scr[m] = m_new

    def far_body(j, carry):
        step(j, None)
        return carry

    lax.fori_loop(0, jnp.maximum(i - 1, 0), far_body, 0)

    @pl.when(i >= 1)
    def _():
        step(i - 1, bias_ref[0, 1])

    step(i, bias_ref[0, 0])

    lam = lam_ref[0]
    o = acc_scr[0] / l_scr[0] - lam * (acc_scr[1] / l_scr[1])
    o_ref[0] = (_rms(o, g_ref[...]) * (1.0 - LAM_INIT)).astype(BF16)


def _attention(proj, lam, bias_tiles, subln_g):
    bsz, seq, _ = proj.shape
    tq = min(TQ, seq)
    nq = seq // tq
    kern = functools.partial(_attn_kernel, tq=tq)
    return pl.pallas_call(
        kern,
        grid=(bsz, N_HEADS, nq),
        in_specs=[pl.BlockSpec(memory_space=pltpu.SMEM),
                  pl.BlockSpec((1, tq, V_DIM), lambda b, h, i: (b, i, COL_Q // V_DIM + h)),
                  pl.BlockSpec((1, seq, V_DIM), lambda b, h, i: (b, 0, COL_K // V_DIM + h)),
                  pl.BlockSpec((1, seq, V_DIM), lambda b, h, i: (b, 0, COL_V // V_DIM + h)),
                  pl.BlockSpec((1, 2, tq, tq), lambda b, h, i: (h, 0, 0, 0)),
                  pl.BlockSpec((1, V_DIM), lambda b, h, i: (0, 0))],
        out_specs=pl.BlockSpec((1, tq, V_DIM), lambda b, h, i: (b, i, h)),
        out_shape=jax.ShapeDtypeStruct((bsz, seq, N_HEADS * V_DIM), BF16),
        scratch_shapes=[pltpu.VMEM((2, tq, 1), F32), pltpu.VMEM((2, tq, 1), F32),
                        pltpu.VMEM((2, tq, V_DIM), F32)],
        compiler_params=pltpu.CompilerParams(
            dimension_semantics=("arbitrary", "arbitrary", "arbitrary"),
            vmem_limit_bytes=VMEM_LIMIT),
        name="attn",
    )(lam, proj, proj, proj, bias_tiles, subln_g.reshape(1, -1))


def _t5_bucket(dist):
    max_exact = N_BUCKETS // 2
    d = jnp.maximum(dist, 1).astype(F32)
    large = max_exact + (jnp.log(d / max_exact) / math.log(MAX_DISTANCE / max_exact)
                         * (N_BUCKETS - max_exact)).astype(jnp.int32)
    large = jnp.minimum(large, N_BUCKETS - 1)
    return jnp.where(dist < max_exact, dist, large)


def _bias_tiles(rel_table, tq):
    assert tq >= MAX_DISTANCE
    r = jnp.arange(tq, dtype=jnp.int32)
    rel0 = r[:, None] - r[None, :]
    rel = jnp.stack([rel0, rel0 + tq])
    table = (rel_table - rel_table[N_BUCKETS - 1:N_BUCKETS, :]).astype(F32)
    bias = table[_t5_bucket(jnp.maximum(rel, 0))]
    bias = jnp.where((rel >= 0)[..., None], bias, MASK_NEG)
    return bias.transpose(3, 0, 1, 2)


def _post_kernel(att_ref, ga_ref, gb_ref, cb_ref, cc_ref, cu_ref, x_ref, mod_ref,
                 cw_ref, cbias_ref, woc_ref, woa_ref, wmg_ref, g2_ref, wr_ref, br_ref,
                 x1_ref, hp_ref, idx_ref, tw_ref, cnt_ref,
                 ubuf, carry, *, tm, tiles_per_seq):
    i = pl.program_id(0)

    @pl.when(i % tiles_per_seq == 0)
    def _():
        carry[...] = jnp.zeros(carry.shape, F32)

    @pl.when(i == 0)
    def _():
        cnt_ref[...] = jnp.zeros(cnt_ref.shape, F32)

    u = cc_ref[...].astype(F32) * cu_ref[...].astype(F32)
    ubuf[0:8, :] = carry[...]
    ubuf[8:8 + tm, :] = u
    carry[...] = u[tm - 8:tm, :]
    conv = (cbias_ref[...] + ubuf[6:6 + tm, :] * cw_ref[0:1, :]
            + ubuf[7:7 + tm, :] * cw_ref[1:2, :] + u * cw_ref[2:3, :])
    ya = jnp.dot((cb_ref[...].astype(F32) * conv).astype(BF16), woc_ref[...],
                 preferred_element_type=F32)
    yb = jnp.dot(att_ref[...], woa_ref[...], preferred_element_type=F32)
    merged = _sigmoid(ga_ref[...].astype(F32)) * ya + _sigmoid(gb_ref[...].astype(F32)) * yb
    mix = jnp.dot(merged.astype(BF16), wmg_ref[...], preferred_element_type=F32)
    x1 = x_ref[...] + mod_ref[0, 2:3, :] * mix
    x1_ref[...] = x1

    h2 = _rms(x1, g2_ref[...]) * (1.0 + mod_ref[0, 4:5, :]) + mod_ref[0, 3:4, :]
    h_hi = h2.astype(BF16)
    h_hi32 = h_hi.astype(F32)
    h_lo = (h2 - h_hi32).astype(BF16)

    part = jnp.dot(h_hi, wr_ref[...], preferred_element_type=F32)
    logits = (part[:, :N_EXPERTS] + part[:, N_EXPERTS:]
              + jnp.dot(h_lo, wr_ref[:, :N_EXPERTS], preferred_element_type=F32)
              + br_ref[...])

    lane = lax.broadcasted_iota(jnp.int32, logits.shape, 1)
    work = logits
    sel_l, sel_i = [], []
    member = jnp.zeros(logits.shape, F32)
    for _ in range(TOP_K):
        mx = jnp.max(work, axis=-1, keepdims=True)
        ix = jnp.min(jnp.where(work == mx, lane, N_EXPERTS), axis=-1, keepdims=True)
        hit = lane == ix
        member = member + hit.astype(F32)
        work = jnp.where(hit, -jnp.inf, work)
        sel_l.append(mx)
        sel_i.append(ix)
    ex = [jnp.exp(l - sel_l[0]) for l in sel_l]
    den = ex[0] + ex[1] + ex[2] + ex[3]
    tw_ref[...] = jnp.concatenate([e / den for e in ex], axis=1)
    idx_ref[...] = jnp.concatenate(sel_i, axis=1)
    cnt_ref[...] += jnp.sum(member, axis=0, keepdims=True)

    bits = pltpu.bitcast(h_hi32, jnp.uint32)
    half = D_MODEL // 2
    hp_ref[...] = (bits[:, half:] & jnp.uint32(0xFFFF0000)) | (bits[:, :half] >> 16)


def _post(att, proj, x2, mod3, conv_w, conv_b, woc_b, woa_b, wmg_b, norm2_g, wr_split, b_router, seq):
    n = x2.shape[0]
    tm = min(TM_POST, seq)
    tiles_per_seq = seq // tm
    kern = functools.partial(_post_kernel, tm=tm, tiles_per_seq=tiles_per_seq)
    row = lambda c: (lambda i: (i, c))
    const = lambda i: (0, 0)
    return pl.pallas_call(
        kern,
        grid=(n // tm,),
        in_specs=[pl.BlockSpec((tm, D_MODEL), row(0)),
                  pl.BlockSpec((tm, D_MODEL), row(COL_GA // D_MODEL)),
                  pl.BlockSpec((tm, D_MODEL), row(COL_GB // D_MODEL)),
                  pl.BlockSpec((tm, CONV_WIDTH), row(COL_CB // CONV_WIDTH)),
                  pl.BlockSpec((tm, CONV_WIDTH), row(COL_CC // CONV_WIDTH)),
                  pl.BlockSpec((tm, CONV_WIDTH), row(COL_CU // CONV_WIDTH)),
                  pl.BlockSpec((tm, D_MODEL), row(0)),
                  pl.BlockSpec((1, 6, D_MODEL), lambda i: (i // tiles_per_seq, 0, 0)),
                  pl.BlockSpec((CONV_K, CONV_WIDTH), const),
                  pl.BlockSpec((1, CONV_WIDTH), const),
                  pl.BlockSpec((CONV_WIDTH, D_MODEL), const),
                  pl.BlockSpec((D_MODEL, D_MODEL), const),
                  pl.BlockSpec((D_MODEL, D_MODEL), const),
                  pl.BlockSpec((1, D_MODEL), const),
                  pl.BlockSpec((D_MODEL, 2 * N_EXPERTS), const),
                  pl.BlockSpec((1, N_EXPERTS), const)],
        out_specs=[pl.BlockSpec((tm, D_MODEL), row(0)),
                   pl.BlockSpec((tm, D_MODEL // 2), row(0)),
                   pl.BlockSpec((tm, TOP_K), row(0)),
                   pl.BlockSpec((tm, TOP_K), row(0)),
                   pl.BlockSpec((1, N_EXPERTS), const)],
        out_shape=[jax.ShapeDtypeStruct((n, D_MODEL), F32),
                   jax.ShapeDtypeStruct((n, D_MODEL // 2), jnp.uint32),
                   jax.ShapeDtypeStruct((n, TOP_K), jnp.int32),
                   jax.ShapeDtypeStruct((n, TOP_K), F32),
                   jax.ShapeDtypeStruct((1, N_EXPERTS), F32)],
        scratch_shapes=[pltpu.VMEM((tm + 8, CONV_WIDTH), F32), pltpu.VMEM((8, CONV_WIDTH), F32)],
        compiler_params=pltpu.CompilerParams(dimension_semantics=("arbitrary",),
                                             vmem_limit_bytes=VMEM_LIMIT),
        name="post",
    )(att, proj, proj, proj, proj, proj, x2, mod3, conv_w, conv_b.reshape(1, -1),
      woc_b, woa_b, wmg_b, norm2_g.reshape(1, -1), wr_split, b_router.reshape(1, -1))


def _scan_kernel(idx_ref, pstart_ref, dest_ref, carry, *, t):
    @pl.when(pl.program_id(0) == 0)
    def _():
        carry[...] = jnp.zeros(carry.shape, F32)

    idx = idx_ref[...]
    lane = lax.broadcasted_iota(jnp.int32, (t, N_EXPERTS), 1)
    hits = [lane == idx[:, k:k + 1] for k in range(TOP_K)]
    member = hits[0].astype(F32)
    for k in range(1, TOP_K):
        member = member + hits[k].astype(F32)
    r = lax.broadcasted_iota(jnp.int32, (t, t), 0)
    c = lax.broadcasted_iota(jnp.int32, (t, t), 1)
    earlier = jnp.where(c < r, 1.0, 0.0).astype(BF16)
    rank = jnp.dot(earlier, member.astype(BF16), preferred_element_type=F32) + carry[...]
    dest_e = pstart_ref[...] + rank
    cols = [jnp.sum(jnp.where(hits[k], dest_e, 0.0), axis=-1, keepdims=True) for k in range(TOP_K)]
    dest_ref[...] = jnp.concatenate(cols, axis=1).astype(jnp.int32)
    carry[...] += jnp.sum(member, axis=0, keepdims=True)


def _scan(idx, pstart):
    n = idx.shape[0]
    t = min(T_SCAN, n)
    return pl.pallas_call(
        functools.partial(_scan_kernel, t=t),
        grid=(n // t,),
        in_specs=[pl.BlockSpec((t, TOP_K), lambda i: (i, 0)),
                  pl.BlockSpec((1, N_EXPERTS), lambda i: (0, 0))],
        out_specs=pl.BlockSpec((t, TOP_K), lambda i: (i, 0)),
        out_shape=jax.ShapeDtypeStruct((n, TOP_K), jnp.int32),
        scratch_shapes=[pltpu.VMEM((1, N_EXPERTS), F32)],
        compiler_params=pltpu.CompilerParams(dimension_semantics=("arbitrary",)),
        name="scan",
    )(idx, pstart)


def _row_copy(src_ref, src_row, dst_ref, dst_row, sem):
    return pltpu.make_async_copy(src_ref.at[pl.ds(src_row, 1)], dst_ref.at[pl.ds(dst_row, 1)], sem)


def _dispatch_kernel(dest_ref, hp_ref, zeros_ref, xs_ref, sem, *, t):
    del zeros_ref

    def issue(tok, carry):
        for k in range(TOP_K):
            _row_copy(hp_ref, tok, xs_ref, dest_ref[tok * TOP_K + k], sem).start()
        return carry

    lax.fori_loop(0, t, issue, 0)

    def drain(tok, carry):
        for k in range(TOP_K):
            _row_copy(hp_ref, tok, xs_ref, dest_ref[tok * TOP_K + k], sem).wait()
        return carry

    lax.fori_loop(0, t, drain, 0)


def _dispatch(dest_flat, hp, rows):
    n, width = hp.shape
    t = min(T_ROW, n)
    zeros = jnp.zeros((rows, width), hp.dtype)
    return pl.pallas_call(
        functools.partial(_dispatch_kernel, t=t),
        grid=(n // t,),
        in_specs=[pl.BlockSpec((t * TOP_K,), lambda i: (i,), memory_space=pltpu.SMEM),
                  pl.BlockSpec((t, width), lambda i: (i, 0)),
                  pl.BlockSpec(memory_space=pl.ANY)],
        out_specs=pl.BlockSpec(memory_space=pl.ANY),
        out_shape=jax.ShapeDtypeStruct((rows, width), hp.dtype),
        scratch_shapes=[pltpu.SemaphoreType.DMA],
        input_output_aliases={2: 0},
        compiler_params=pltpu.CompilerParams(dimension_semantics=("arbitrary",),
                                             has_side_effects=True),
        name="dispatch",
    )(dest_flat, hp, zeros)


def _expert_kernel(be_ref, xs_ref, wg_ref, wl_ref, bg_ref, bl_ref, wd_ref, bd_ref, y_ref):
    del be_ref
    xp = xs_ref[...]
    lo = pltpu.bitcast(xp << 16, F32)
    hi = pltpu.bitcast(xp & jnp.uint32(0xFFFF0000), F32)
    xb = jnp.concatenate([lo, hi], axis=1).astype(BF16)
    g = jnp.dot(xb, wg_ref[0], preferred_element_type=F32) + bg_ref[0]
    l = jnp.dot(xb, wl_ref[0], preferred_element_type=F32) + bl_ref[0]
    glu = jnp.minimum(g, SWIGLU_LIMIT)
    lin = jnp.clip(l, -SWIGLU_LIMIT, SWIGLU_LIMIT)
    act = glu * _sigmoid(SWIGLU_ALPHA * glu) * (lin + 1.0)
    y_ref[...] = jnp.dot(act.astype(BF16), wd_ref[0], preferred_element_type=F32) + bd_ref[0]


def _experts(block_e, xs, wg, wl, bg, bl, wd, bd):
    rows, width = xs.shape
    d_exp = wg.shape[2]
    n_blocks = rows // ROW_BLOCK
    wsel = lambda i, be: (be[i], 0, 0)
    grid_spec = pltpu.PrefetchScalarGridSpec(
        num_scalar_prefetch=1,
        grid=(n_blocks,),
        in_specs=[pl.BlockSpec((ROW_BLOCK, width), lambda i, be: (i, 0)),
                  pl.BlockSpec((1, D_MODEL, d_exp), wsel),
                  pl.BlockSpec((1, D_MODEL, d_exp), wsel),
                  pl.BlockSpec((1, 1, d_exp), wsel),
                  pl.BlockSpec((1, 1, d_exp), wsel),
                  pl.BlockSpec((1, d_exp, D_MODEL), wsel),
                  pl.BlockSpec((1, 1, D_MODEL), wsel)],
        out_specs=pl.BlockSpec((ROW_BLOCK, D_MODEL), lambda i, be: (i, 0)),
    )
    return pl.pallas_call(
        _expert_kernel,
        grid_spec=grid_spec,
        out_shape=jax.ShapeDtypeStruct((rows, D_MODEL), F32),
        compiler_params=pltpu.CompilerParams(dimension_semantics=("arbitrary",),
                                             vmem_limit_bytes=VMEM_LIMIT),
        name="experts",
    )(block_e, xs, wg, wl, bg, bl, wd, bd)


def _combine_kernel(dest_ref, y_ref, x1_ref, tw_ref, mod_ref, gf_ref, o_ref, buf, sem, *, t):
    def issue(tok, carry):
        for k in range(TOP_K):
            _row_copy(y_ref, dest_ref[tok * TOP_K + k], buf.at[k], tok, sem).start()
        return carry

    lax.fori_loop(0, t, issue, 0)

    def drain(tok, carry):
        for k in range(TOP_K):
            _row_copy(y_ref, dest_ref[tok * TOP_K + k], buf.at[k], tok, sem).wait()
        return carry

    lax.fori_loop(0, t, drain, 0)

    tw = tw_ref[...]
    moe = tw[:, 0:1] * buf[0]
    for k in range(1, TOP_K):
        moe = moe + tw[:, k:k + 1] * buf[k]
    xo = x1_ref[...] + mod_ref[0, 5:6, :] * moe
    o_ref[...] = _rms(xo, gf_ref[...])


def _combine(dest_flat, y, x1, tw, mod3, normf_g, seq):
    n = x1.shape[0]
    t = min(T_ROW, seq)
    tiles_per_seq = seq // t
    return pl.pallas_call(
        functools.partial(_combine_kernel, t=t),
        grid=(n // t,),
        in_specs=[pl.BlockSpec((t * TOP_K,), lambda i: (i,), memory_space=pltpu.SMEM),
                  pl.BlockSpec(memory_space=pl.ANY),
                  pl.BlockSpec((t, D_MODEL), lambda i: (i, 0)),
                  pl.BlockSpec((t, TOP_K), lambda i: (i, 0)),
                  pl.BlockSpec((1, 6, D_MODEL), lambda i: (i // tiles_per_seq, 0, 0)),
                  pl.BlockSpec((1, D_MODEL), lambda i: (0, 0))],
        out_specs=pl.BlockSpec((t, D_MODEL), lambda i: (i, 0)),
        out_shape=jax.ShapeDtypeStruct((n, D_MODEL), F32),
        scratch_shapes=[pltpu.VMEM((TOP_K, t, D_MODEL), F32), pltpu.SemaphoreType.DMA],
        compiler_params=pltpu.CompilerParams(dimension_semantics=("arbitrary",)),
        name="combine",
    )(dest_flat, y, x1, tw, mod3, normf_g.reshape(1, -1))


def _in_proj_weight(w_in):
    conv_w = 3 * CONV_WIDTH
    return jnp.concatenate([w_in[:, conv_w:], w_in[:, :conv_w]], axis=1).astype(BF16)


def kernel(x, c, w_ada, b_ada, norm1_g, norm2_g, w_in, conv_w, conv_b, w_out_conv, lambda_q1, lambda_k1, lambda_q2, lambda_k2, subln_g, w_o_attn, w_merge, w_router, b_router, w_gate_up, b_gate_up, w_down, b_down, rel_bias_table, normf_g):
    bsz, seq, _ = x.shape
    n = bsz * seq
    assert w_ada.shape[0] == 1, "single layer"
    assert seq % min(TQ, seq) == 0 and n % T_ROW == 0 and (n * TOP_K) % ROW_BLOCK == 0

    mod, lam = _ada(c, w_ada[0], b_ada[0], lambda_q1[0], lambda_k1[0], lambda_q2[0], lambda_k2[0])
    mod3 = mod.reshape(bsz, 6, D_MODEL)

    proj = _inproj(x, mod3, norm1_g[0], _in_proj_weight(w_in[0]))
    att = _attention(proj, lam[0, :1], _bias_tiles(rel_bias_table, min(TQ, seq)), subln_g[0])

    wr = w_router[0]
    wr_hi = wr.astype(BF16)
    wr_lo = (wr - wr_hi.astype(F32)).astype(BF16)
    x1, hp, idx, tw, counts = _post(
        att.reshape(n, -1), proj.reshape(n, -1), x.reshape(n, -1), mod3, conv_w[0], conv_b[0],
        w_out_conv[0].astype(BF16), w_o_attn[0].astype(BF16), w_merge[0].astype(BF16),
        norm2_g[0], jnp.concatenate([wr_hi, wr_lo], axis=1), b_router[0], seq)

    cnt = counts[0].astype(jnp.int32)
    padded = (cnt + ROW_BLOCK - 1) // ROW_BLOCK * ROW_BLOCK
    pend = jnp.cumsum(padded)
    pstart = pend - padded
    n_blocks = n * TOP_K // ROW_BLOCK + N_EXPERTS
    rows = n_blocks * ROW_BLOCK
    block_e = jnp.minimum(
        jnp.searchsorted(pend, jnp.arange(n_blocks, dtype=jnp.int32) * ROW_BLOCK, side='right'),
        N_EXPERTS - 1).astype(jnp.int32)

    dest = _scan(idx, pstart.astype(F32).reshape(1, -1)).reshape(-1)
    xs = _dispatch(dest, hp, rows)

    wgu = w_gate_up[0]
    bgu = b_gate_up[0]
    y = _experts(block_e, xs,
                 wgu[:, :, 0::2].astype(BF16), wgu[:, :, 1::2].astype(BF16),
                 bgu[:, None, 0::2], bgu[:, None, 1::2],
                 w_down[0].astype(BF16), b_down[0][:, None, :])

    out = _combine(dest, y, x1, tw, mod3, normf_g, seq)
    return out.reshape(bsz, seq, D_MODEL)
```

```python
import functools
import math

import jax
import jax.numpy as jnp
from jax import lax
from jax.experimental import pallas as pl
from jax.experimental.pallas import tpu as pltpu

F32 = jnp.float32
BF16 = jnp.bfloat16

D_MODEL = 1024
CONV_WIDTH = 512
CONV_K = 3
N_HEADS = 8
HEAD_DIM = 64
V_DIM = 2 * HEAD_DIM
N_BUCKETS = 32
MAX_DISTANCE = 128
N_EXPERTS = 32
TOP_K = 4
SWIGLU_ALPHA = 1.702
SWIGLU_LIMIT = 7.0
EPS = 1e-6
LAM_INIT = 0.8 - 0.6 * math.exp(-0.3 * 0)

D_NAT = 3 * CONV_WIDTH + 3 * D_MODEL
COL_K, COL_GA, COL_GB = 0, 1024, 2048
COL_CB, COL_CC, COL_CU = 3072, 3584, 4096
D_TR = 2 * D_MODEL
ROW_QT, ROW_VT = 0, 1024

TM_IN = 1024
TN_IN = 1152
TQ = 512
TM_POST = 512
T_SCAN = 512
T_ROW = 256
ROW_BLOCK = 512
MASK_NEG = -1e30
VMEM_LIMIT = 56 * 1024 * 1024


def _sigmoid(x):
    return 1.0 / (1.0 + jnp.exp(-x))


def _rms(x, g):
    return x * lax.rsqrt(jnp.mean(x * x, axis=-1, keepdims=True) + EPS) * g


def _ada_kernel(c_ref, w_ref, b_ref, lq1_ref, lk1_ref, lq2_ref, lk2_ref, mod_ref, lam_ref):
    c = c_ref[...]
    s = c * _sigmoid(c)
    mod_ref[...] = jnp.dot(s, w_ref[...], precision=lax.Precision.HIGHEST,
                           preferred_element_type=F32) + b_ref[...]
    a1 = jnp.sum(lq1_ref[...] * lk1_ref[...], axis=-1, keepdims=True)
    a2 = jnp.sum(lq2_ref[...] * lk2_ref[...], axis=-1, keepdims=True)
    lam_ref[...] = jnp.broadcast_to(jnp.exp(a1) - jnp.exp(a2) + LAM_INIT, lam_ref.shape)


def _ada(c, w_ada, b_ada, lq1, lk1, lq2, lk2):
    bsz = c.shape[0]
    n_chunks = w_ada.shape[1] // D_MODEL
    vec = pl.BlockSpec((1, HEAD_DIM), lambda j: (0, 0))
    return pl.pallas_call(
        _ada_kernel,
        grid=(n_chunks,),
        in_specs=[pl.BlockSpec((bsz, D_MODEL), lambda j: (0, 0)),
                  pl.BlockSpec((D_MODEL, D_MODEL), lambda j: (0, j)),
                  pl.BlockSpec((1, D_MODEL), lambda j: (0, j)),
                  vec, vec, vec, vec],
        out_specs=[pl.BlockSpec((bsz, D_MODEL), lambda j: (0, j)),
                   pl.BlockSpec((1, 128), lambda j: (0, 0))],
        out_shape=[jax.ShapeDtypeStruct((bsz, n_chunks * D_MODEL), F32),
                   jax.ShapeDtypeStruct((1, 128), F32)],
        name="ada",
    )(c, w_ada, b_ada.reshape(1, -1), lq1.reshape(1, -1), lk1.reshape(1, -1),
      lq2.reshape(1, -1), lk2.reshape(1, -1))


def _inproj_kernel(x_ref, mod_ref, g_ref, w_ref, wt_ref, o_ref, ot_ref, h_scr, ht_scr, *, n_nat):
    j = pl.program_id(2)

    @pl.when(j == 0)
    def _():
        h = _rms(x_ref[0], g_ref[...]) * (1.0 + mod_ref[0, 1:2, :]) + mod_ref[0, 0:1, :]
        h_scr[...] = h.astype(BF16)
        ht_scr[...] = h.T.astype(BF16)

    @pl.when(j < n_nat)
    def _():
        o_ref[0] = jnp.dot(h_scr[...], w_ref[...], preferred_element_type=F32).astype(BF16)

    @pl.when(j == n_nat)
    def _():
        ot_ref[0] = jnp.dot(wt_ref[...], ht_scr[...], preferred_element_type=F32).astype(BF16)


def _inproj(x, mod3, norm1_g, w_nat, w_tr):
    bsz, seq, _ = x.shape
    tm = min(TM_IN, seq)
    n_nat = D_NAT // TN_IN
    nat_j = lambda j: jnp.minimum(j, n_nat - 1)
    return pl.pallas_call(
        functools.partial(_inproj_kernel, n_nat=n_nat),
        grid=(bsz, seq // tm, n_nat + 1),
        in_specs=[pl.BlockSpec((1, tm, D_MODEL), lambda b, i, j: (b, i, 0)),
                  pl.BlockSpec((1, 6, D_MODEL), lambda b, i, j: (b, 0, 0)),
                  pl.BlockSpec((1, D_MODEL), lambda b, i, j: (0, 0)),
                  pl.BlockSpec((D_MODEL, TN_IN), lambda b, i, j: (0, nat_j(j))),
                  pl.BlockSpec((D_TR, D_MODEL), lambda b, i, j: (0, 0))],
        out_specs=[pl.BlockSpec((1, tm, TN_IN), lambda b, i, j: (b, i, nat_j(j))),
                   pl.BlockSpec((1, D_TR, tm), lambda b, i, j: (b, 0, i))],
        out_shape=[jax.ShapeDtypeStruct((bsz, seq, D_NAT), BF16),
                   jax.ShapeDtypeStruct((bsz, D_TR, seq), BF16)],
        scratch_shapes=[pltpu.VMEM((tm, D_MODEL), BF16), pltpu.VMEM((D_MODEL, tm), BF16)],
        compiler_params=pltpu.CompilerParams(
            dimension_semantics=("arbitrary", "arbitrary", "arbitrary"),
            vmem_limit_bytes=VMEM_LIMIT),
        name="inproj",
    )(x, mod3, norm1_g.reshape(1, -1), w_nat, w_tr)


def _attn_kernel(lam_ref, qt_ref, k_ref, vt_ref, bias_ref, g_ref, o_ref, m_scr, l_scr, acc_scr, *, tq):
    i = pl.program_id(2)
    qt = qt_ref[0] * jnp.asarray(HEAD_DIM ** -0.5, BF16)
    row = lax.broadcasted_iota(jnp.int32, qt.shape, 0)
    zero = jnp.zeros_like(qt)
    qt_maps = (jnp.where(row < HEAD_DIM, qt, zero), jnp.where(row >= HEAD_DIM, qt, zero))

    m_scr[...] = jnp.full(m_scr.shape, MASK_NEG, F32)
    l_scr[...] = jnp.zeros(l_scr.shape, F32)
    acc_scr[...] = jnp.zeros(acc_scr.shape, F32)

    def step(j, bias):
        off = pl.multiple_of(j * tq, tq)
        k = k_ref[0, pl.ds(off, tq), :]
        vt = vt_ref[0, :, pl.ds(off, tq)]
        for m in range(2):
            s = jnp.dot(k, qt_maps[m], preferred_element_type=F32)
            if bias is not None:
                s = s + bias
            m_prev = m_scr[m]
            m_new = jnp.maximum(m_prev, jnp.max(s, axis=0, keepdims=True))
            alpha = jnp.exp(m_prev - m_new)
            p = jnp.exp(s - m_new)
            l_scr[m] = alpha * l_scr[m] + jnp.sum(p, axis=0, keepdims=True)
            acc_scr[m] = alpha * acc_scr[m] + jnp.dot(vt, p.astype(BF16), preferred_element_type=F32)
            m_scr[m] = m_new

    def far_body(j, carry):
        step(j, None)
        return carry

    lax.fori_loop(0, jnp.maximum(i - 1, 0), far_body, 0)

    @pl.when(i >= 1)
    def _():
        step(i - 1, bias_ref[0, 1])

    step(i, bias_ref[0, 0])

    lam = lam_ref[0]
    ot = acc_scr[0] / l_scr[0] - lam * (acc_scr[1] / l_scr[1])
    ot = ot * lax.rsqrt(jnp.mean(ot * ot, axis=0, keepdims=True) + EPS) * g_ref[...]
    o_ref[0] = (ot * (1.0 - LAM_INIT)).T.astype(BF16)


def _attention(proj, proj_t, lam, bias_tiles, subln_g):
    bsz, seq, _ = proj.shape
    tq = min(TQ, seq)
    nq = seq // tq
    kern = functools.partial(_attn_kernel, tq=tq)
    return pl.pallas_call(
        kern,
        grid=(bsz, N_HEADS, nq),
        in_specs=[pl.BlockSpec(memory_space=pltpu.SMEM),
                  pl.BlockSpec((1, V_DIM, tq), lambda b, h, i: (b, ROW_QT // V_DIM + h, i)),
                  pl.BlockSpec((1, seq, V_DIM), lambda b, h, i: (b, 0, COL_K // V_DIM + h)),
                  pl.BlockSpec((1, V_DIM, seq), lambda b, h, i: (b, ROW_VT // V_DIM + h, 0)),
                  pl.BlockSpec((1, 2, tq, tq), lambda b, h, i: (h, 0, 0, 0)),
                  pl.BlockSpec((V_DIM, 1), lambda b, h, i: (0, 0))],
        out_specs=pl.BlockSpec((1, tq, V_DIM), lambda b, h, i: (b, i, h)),
        out_shape=jax.ShapeDtypeStruct((bsz, seq, N_HEADS * V_DIM), BF16),
        scratch_shapes=[pltpu.VMEM((2, 1, tq), F32), pltpu.VMEM((2, 1, tq), F32),
                        pltpu.VMEM((2, V_DIM, tq), F32)],
        compiler_params=pltpu.CompilerParams(
            dimension_semantics=("arbitrary", "arbitrary", "arbitrary"),
            vmem_limit_bytes=VMEM_LIMIT),
        name="attn",
    )(lam, proj_t, proj, proj_t, bias_tiles, subln_g.reshape(-1, 1))


def _t5_bucket(dist):
    max_exact = N_BUCKETS // 2
    d = jnp.maximum(dist, 1).astype(F32)
    large = max_exact + (jnp.log(d / max_exact) / math.log(MAX_DISTANCE / max_exact)
                         * (N_BUCKETS - max_exact)).astype(jnp.int32)
    large = jnp.minimum(large, N_BUCKETS - 1)
    return jnp.where(dist < max_exact, dist, large)


def _bias_tiles(rel_table, tq):
    assert tq >= MAX_DISTANCE
    r = jnp.arange(tq, dtype=jnp.int32)
    rel0 = r[None, :] - r[:, None]
    rel = jnp.stack([rel0, rel0 + tq])
    table = (rel_table - rel_table[N_BUCKETS - 1:N_BUCKETS, :]).astype(F32)
    bucket = _t5_bucket(jnp.maximum(rel, 0))
    onehot = (bucket[None] == jnp.arange(N_BUCKETS, dtype=jnp.int32)[:, None, None, None]).astype(F32)
    bias = jnp.einsum('kh,kabc->habc', table, onehot, precision=lax.Precision.HIGHEST)
    return jnp.where((rel >= 0)[None], bias, MASK_NEG)


def _post_kernel(att_ref, ga_ref, gb_ref, cb_ref, cc_ref, cu_ref, x_ref, mod_ref,
                 cw_ref, cbias_ref, woc_ref, woa_ref, wmg_ref, g2_ref, wr_ref, br_ref,
                 x1_ref, hp_ref, idx_ref, tw_ref, cnt_ref,
                 ubuf, carry, *, tm, tiles_per_seq):
    i = pl.program_id(0)

    @pl.when(i % tiles_per_seq == 0)
    def _():
        carry[...] = jnp.zeros(carry.shape, F32)

    @pl.when(i == 0)
    def _():
        cnt_ref[...] = jnp.zeros(cnt_ref.shape, F32)

    u = cc_ref[...].astype(F32) * cu_ref[...].astype(F32)
    ubuf[0:8, :] = carry[...]
    ubuf[8:8 + tm, :] = u
    carry[...] = u[tm - 8:tm, :]
    conv = (cbias_ref[...] + ubuf[6:6 + tm, :] * cw_ref[0:1, :]
            + ubuf[7:7 + tm, :] * cw_ref[1:2, :] + u * cw_ref[2:3, :])
    ya = jnp.dot((cb_ref[...].astype(F32) * conv).astype(BF16), woc_ref[...],
                 preferred_element_type=F32)
    yb = jnp.dot(att_ref[...], woa_ref[...], preferred_element_type=F32)
    merged = _sigmoid(ga_ref[...].astype(F32)) * ya + _sigmoid(gb_ref[...].astype(F32)) * yb
    mix = jnp.dot(merged.astype(BF16), wmg_ref[...], preferred_element_type=F32)
    x1 = x_ref[...] + mod_ref[0, 2:3, :] * mix
    x1_ref[...] = x1

    h2 = _rms(x1, g2_ref[...]) * (1.0 + mod_ref[0, 4:5, :]) + mod_ref[0, 3:4, :]
    h_hi = h2.astype(BF16)
    h_hi32 = h_hi.astype(F32)
    h_lo = (h2 - h_hi32).astype(BF16)

    part = jnp.dot(h_hi, wr_ref[...], preferred_element_type=F32)
    logits = (part[:, :N_EXPERTS] + part[:, N_EXPERTS:]
              + jnp.dot(h_lo, wr_ref[:, :N_EXPERTS], preferred_element_type=F32)
              + br_ref[...])

    lane = lax.broadcasted_iota(jnp.int32, logits.shape, 1)
    work = logits
    sel_l, sel_i = [], []
    member = jnp.zeros(logits.shape, F32)
    for _ in range(TOP_K):
        mx = jnp.max(work, axis=-1, keepdims=True)
        ix = jnp.min(jnp.where(work == mx, lane, N_EXPERTS), axis=-1, keepdims=True)
        hit = lane == ix
        member = member + hit.astype(F32)
        work = jnp.where(hit, -jnp.inf, work)
        sel_l.append(mx)
        sel_i.append(ix)
    ex = [jnp.exp(l - sel_l[0]) for l in sel_l]
    den = ex[0] + ex[1] + ex[2] + ex[3]
    tw_ref[...] = jnp.concatenate([e / den for e in ex], axis=1)
    idx_ref[...] = jnp.concatenate(sel_i, axis=1)
    cnt_ref[...] += jnp.sum(member, axis=0, keepdims=True)

    bits = pltpu.bitcast(h_hi32, jnp.uint32)
    half = D_MODEL // 2
    hp_ref[...] = (bits[:, half:] & jnp.uint32(0xFFFF0000)) | (bits[:, :half] >> 16)


def _post(att, proj, x2, mod3, conv_w, conv_b, woc_b, woa_b, wmg_b, norm2_g, wr_split, b_router, seq):
    n = x2.shape[0]
    tm = min(TM_POST, seq)
    tiles_per_seq = seq // tm
    kern = functools.partial(_post_kernel, tm=tm, tiles_per_seq=tiles_per_seq)
    row = lambda c: (lambda i: (i, c))
    const = lambda i: (0, 0)
    return pl.pallas_call(
        kern,
        grid=(n // tm,),
        in_specs=[pl.BlockSpec((tm, D_MODEL), row(0)),
                  pl.BlockSpec((tm, D_MODEL), row(COL_GA // D_MODEL)),
                  pl.BlockSpec((tm, D_MODEL), row(COL_GB // D_MODEL)),
                  pl.BlockSpec((tm, CONV_WIDTH), row(COL_CB // CONV_WIDTH)),
                  pl.BlockSpec((tm, CONV_WIDTH), row(COL_CC // CONV_WIDTH)),
                  pl.BlockSpec((tm, CONV_WIDTH), row(COL_CU // CONV_WIDTH)),
                  pl.BlockSpec((tm, D_MODEL), row(0)),
                  pl.BlockSpec((1, 6, D_MODEL), lambda i: (i // tiles_per_seq, 0, 0)),
                  pl.BlockSpec((CONV_K, CONV_WIDTH), const),
                  pl.BlockSpec((1, CONV_WIDTH), const),
                  pl.BlockSpec((CONV_WIDTH, D_MODEL), const),
                  pl.BlockSpec((D_MODEL, D_MODEL), const),
                  pl.BlockSpec((D_MODEL, D_MODEL), const),
                  pl.BlockSpec((1, D_MODEL), const),
                  pl.BlockSpec((D_MODEL, 2 * N_EXPERTS), const),
                  pl.BlockSpec((1, N_EXPERTS), const)],
        out_specs=[pl.BlockSpec((tm, D_MODEL), row(0)),
                   pl.BlockSpec((tm, D_MODEL // 2), row(0)),
                   pl.BlockSpec((tm, TOP_K), row(0)),
                   pl.BlockSpec((tm, TOP_K), row(0)),
                   pl.BlockSpec((1, N_EXPERTS), const)],
        out_shape=[jax.ShapeDtypeStruct((n, D_MODEL), F32),
                   jax.ShapeDtypeStruct((n, D_MODEL // 2), jnp.uint32),
                   jax.ShapeDtypeStruct((n, TOP_K), jnp.int32),
                   jax.ShapeDtypeStruct((n, TOP_K), F32),
                   jax.ShapeDtypeStruct((1, N_EXPERTS), F32)],
        scratch_shapes=[pltpu.VMEM((tm + 8, CONV_WIDTH), F32), pltpu.VMEM((8, CONV_WIDTH), F32)],
        compiler_params=pltpu.CompilerParams(dimension_semantics=("arbitrary",),
                                             vmem_limit_bytes=VMEM_LIMIT),
        name="post",
    )(att, proj, proj, proj, proj, proj, x2, mod3, conv_w, conv_b.reshape(1, -1),
      woc_b, woa_b, wmg_b, norm2_g.reshape(1, -1), wr_split, b_router.reshape(1, -1))


def _scan_kernel(idx_ref, pstart_ref, dest_ref, carry, *, t):
    @pl.when(pl.program_id(0) == 0)
    def _():
        carry[...] = jnp.zeros(carry.shape, F32)

    idx = idx_ref[...]
    lane = lax.broadcasted_iota(jnp.int32, (t, N_EXPERTS), 1)
    hits = [lane == idx[:, k:k + 1] for k in range(TOP_K)]
    member = hits[0].astype(F32)
    for k in range(1, TOP_K):
        member = member + hits[k].astype(F32)
    r = lax.broadcasted_iota(jnp.int32, (t, t), 0)
    c = lax.broadcasted_iota(jnp.int32, (t, t), 1)
    earlier = jnp.where(c < r, 1.0, 0.0).astype(BF16)
    rank = jnp.dot(earlier, member.astype(BF16), preferred_element_type=F32) + carry[...]
    dest_e = pstart_ref[...] + rank
    cols = [jnp.sum(jnp.where(hits[k], dest_e, 0.0), axis=-1, keepdims=True) for k in range(TOP_K)]
    dest_ref[...] = jnp.concatenate(cols, axis=1).astype(jnp.int32)
    carry[...] += jnp.sum(member, axis=0, keepdims=True)


def _scan(idx, pstart):
    n = idx.shape[0]
    t = min(T_SCAN, n)
    return pl.pallas_call(
        functools.partial(_scan_kernel, t=t),
        grid=(n // t,),
        in_specs=[pl.BlockSpec((t, TOP_K), lambda i: (i, 0)),
                  pl.BlockSpec((1, N_EXPERTS), lambda i: (0, 0))],
        out_specs=pl.BlockSpec((t, TOP_K), lambda i: (i, 0)),
        out_shape=jax.ShapeDtypeStruct((n, TOP_K), jnp.int32),
        scratch_shapes=[pltpu.VMEM((1, N_EXPERTS), F32)],
        compiler_params=pltpu.CompilerParams(dimension_semantics=("arbitrary",)),
        name="scan",
    )(idx, pstart)


def _row_copy(src_ref, src_row, dst_ref, dst_row, sem):
    return pltpu.make_async_copy(src_ref.at[pl.ds(src_row, 1)], dst_ref.at[pl.ds(dst_row, 1)], sem)


def _dispatch_kernel(dest_ref, hp_ref, zeros_ref, xs_ref, sem, *, t):
    del zeros_ref

    def issue(tok, carry):
        for k in range(TOP_K):
            _row_copy(hp_ref, tok, xs_ref, dest_ref[tok * TOP_K + k], sem).start()
        return carry

    lax.fori_loop(0, t, issue, 0)

    def drain(tok, carry):
        for k in range(TOP_K):
            _row_copy(hp_ref, tok, xs_ref, dest_ref[tok * TOP_K + k], sem).wait()
        return carry

    lax.fori_loop(0, t, drain, 0)


def _dispatch(dest_flat, hp, rows):
    n, width = hp.shape
    t = min(T_ROW, n)
    zeros = jnp.zeros((rows, width), hp.dtype)
    return pl.pallas_call(
        functools.partial(_dispatch_kernel, t=t),
        grid=(n // t,),
        in_specs=[pl.BlockSpec((t * TOP_K,), lambda i: (i,), memory_space=pltpu.SMEM),
                  pl.BlockSpec((t, width), lambda i: (i, 0)),
                  pl.BlockSpec(memory_space=pl.ANY)],
        out_specs=pl.BlockSpec(memory_space=pl.ANY),
        out_shape=jax.ShapeDtypeStruct((rows, width), hp.dtype),
        scratch_shapes=[pltpu.SemaphoreType.DMA],
        input_output_aliases={2: 0},
        compiler_params=pltpu.CompilerParams(dimension_semantics=("arbitrary",),
                                             has_side_effects=True),
        name="dispatch",
    )(dest_flat, hp, zeros)


def _expert_kernel(be_ref, xs_ref, wgu_ref, bgu_ref, wd_ref, bd_ref, y_ref):
    del be_ref
    xp = xs_ref[...]
    lo = pltpu.bitcast(xp << 16, F32)
    hi = pltpu.bitcast(xp & jnp.uint32(0xFFFF0000), F32)
    xb = jnp.concatenate([lo, hi], axis=1).astype(BF16)
    gu = jnp.dot(xb, wgu_ref[0], preferred_element_type=F32) + bgu_ref[0]
    nxt = pltpu.roll(gu, gu.shape[1] - 1, axis=1)
    glu = jnp.minimum(gu, SWIGLU_LIMIT)
    lin = jnp.clip(nxt, -SWIGLU_LIMIT, SWIGLU_LIMIT)
    act = glu * _sigmoid(SWIGLU_ALPHA * glu) * (lin + 1.0)
    wd = pltpu.bitcast(wd_ref[0], BF16)
    y_ref[...] = jnp.dot(act.astype(BF16), wd, preferred_element_type=F32) + bd_ref[0]


def _experts(block_e, xs, wgu, bgu, wd_words, bd):
    rows, width = xs.shape
    d_gu = wgu.shape[2]
    n_blocks = rows // ROW_BLOCK
    wsel = lambda i, be: (be[i], 0, 0)
    grid_spec = pltpu.PrefetchScalarGridSpec(
        num_scalar_prefetch=1,
        grid=(n_blocks,),
        in_specs=[pl.BlockSpec((ROW_BLOCK, width), lambda i, be: (i, 0)),
                  pl.BlockSpec((1, D_MODEL, d_gu), wsel),
                  pl.BlockSpec((1, 1, d_gu), wsel),
                  pl.BlockSpec((1, d_gu // 2, D_MODEL), wsel),
                  pl.BlockSpec((1, 1, D_MODEL), wsel)],
        out_specs=pl.BlockSpec((ROW_BLOCK, D_MODEL), lambda i, be: (i, 0)),
    )
    return pl.pallas_call(
        _expert_kernel,
        grid_spec=grid_spec,
        out_shape=jax.ShapeDtypeStruct((rows, D_MODEL), F32),
        compiler_params=pltpu.CompilerParams(dimension_semantics=("arbitrary",),
                                             vmem_limit_bytes=VMEM_LIMIT),
        name="experts",
    )(block_e, xs, wgu, bgu, wd_words, bd)


def _combine_kernel(dest_ref, y_ref, x1_ref, tw_ref, mod_ref, gf_ref, o_ref, buf, sem, *, t):
    def issue(tok, carry):
        for k in range(TOP_K):
            _row_copy(y_ref, dest_ref[tok * TOP_K + k], buf.at[k], tok, sem).start()
        return carry

    lax.fori_loop(0, t, issue, 0)

    def drain(tok, carry):
        for k in range(TOP_K):
            _row_copy(y_ref, dest_ref[tok * TOP_K + k], buf.at[k], tok, sem).wait()
        return carry

    lax.fori_loop(0, t, drain, 0)

    tw = tw_ref[...]
    moe = tw[:, 0:1] * buf[0]
    for k in range(1, TOP_K):
        moe = moe + tw[:, k:k + 1] * buf[k]
    xo = x1_ref[...] + mod_ref[0, 5:6, :] * moe
    o_ref[...] = _rms(xo, gf_ref[...])


def _combine(dest_flat, y, x1, tw, mod3, normf_g, seq):
    n = x1.shape[0]
    t = min(T_ROW, seq)
    tiles_per_seq = seq // t
    return pl.pallas_call(
        functools.partial(_combine_kernel, t=t),
        grid=(n // t,),
        in_specs=[pl.BlockSpec((t * TOP_K,), lambda i: (i,), memory_space=pltpu.SMEM),
                  pl.BlockSpec(memory_space=pl.ANY),
                  pl.BlockSpec((t, D_MODEL), lambda i: (i, 0)),
                  pl.BlockSpec((t, TOP_K), lambda i: (i, 0)),
                  pl.BlockSpec((1, 6, D_MODEL), lambda i: (i // tiles_per_seq, 0, 0)),
                  pl.BlockSpec((1, D_MODEL), lambda i: (0, 0))],
        out_specs=pl.BlockSpec((t, D_MODEL), lambda i: (i, 0)),
        out_shape=jax.ShapeDtypeStruct((n, D_MODEL), F32),
        scratch_shapes=[pltpu.VMEM((TOP_K, t, D_MODEL), F32), pltpu.SemaphoreType.DMA],
        compiler_params=pltpu.CompilerParams(dimension_semantics=("arbitrary",)),
        name="combine",
    )(dest_flat, y, x1, tw, mod3, normf_g.reshape(1, -1))


def _in_proj_weights(w_in):
    c0 = 3 * CONV_WIDTH
    q, k, v = (w_in[:, c0 + s * D_MODEL:c0 + (s + 1) * D_MODEL] for s in range(3))
    w_nat = jnp.concatenate([k, w_in[:, c0 + 3 * D_MODEL:], w_in[:, :c0]], axis=1).astype(BF16)
    w_tr = jnp.concatenate([q, v], axis=1).T.astype(BF16)
    return w_nat, w_tr


def kernel(x, c, w_ada, b_ada, norm1_g, norm2_g, w_in, conv_w, conv_b, w_out_conv, lambda_q1, lambda_k1, lambda_q2, lambda_k2, subln_g, w_o_attn, w_merge, w_router, b_router, w_gate_up, b_gate_up, w_down, b_down, rel_bias_table, normf_g):
    bsz, seq, _ = x.shape
    n = bsz * seq
    assert w_ada.shape[0] == 1, "single layer"
    assert seq % min(TQ, seq) == 0 and n % T_ROW == 0 and (n * TOP_K) % ROW_BLOCK == 0

    mod, lam = _ada(c, w_ada[0], b_ada[0], lambda_q1[0], lambda_k1[0], lambda_q2[0], lambda_k2[0])
    mod3 = mod.reshape(bsz, 6, D_MODEL)

    proj, proj_t = _inproj(x, mod3, norm1_g[0], *_in_proj_weights(w_in[0]))
    att = _attention(proj, proj_t, lam[0, :1], _bias_tiles(rel_bias_table, min(TQ, seq)), subln_g[0])

    wr = w_router[0]
    wr_hi = wr.astype(BF16)
    wr_lo = (wr - wr_hi.astype(F32)).astype(BF16)
    x1, hp, idx, tw, counts = _post(
        att.reshape(n, -1), proj.reshape(n, -1), x.reshape(n, -1), mod3, conv_w[0], conv_b[0],
        w_out_conv[0].astype(BF16), w_o_attn[0].astype(BF16), w_merge[0].astype(BF16),
        norm2_g[0], jnp.concatenate([wr_hi, wr_lo], axis=1), b_router[0], seq)

    cnt = counts[0].astype(jnp.int32)
    padded = (cnt + ROW_BLOCK - 1) // ROW_BLOCK * ROW_BLOCK
    pend = jnp.cumsum(padded)
    pstart = pend - padded
    n_blocks = n * TOP_K // ROW_BLOCK + N_EXPERTS
    rows = n_blocks * ROW_BLOCK
    block_start = jnp.arange(n_blocks, dtype=jnp.int32) * ROW_BLOCK
    block_e = jnp.minimum(jnp.sum(pend[None, :] <= block_start[:, None], axis=1),
                          N_EXPERTS - 1).astype(jnp.int32)

    dest = _scan(idx, pstart.astype(F32).reshape(1, -1)).reshape(-1)
    xs = _dispatch(dest, hp, rows)

    wd_words = lax.bitcast_convert_type(w_down[0].astype(BF16), jnp.uint16).astype(jnp.uint32)
    y = _experts(block_e, xs, w_gate_up[0].astype(BF16), b_gate_up[0][:, None, :],
                 wd_words, b_down[0][:, None, :])

    out = _combine(dest, y, x1, tw, mod3, normf_g, seq)
    return out.reshape(bsz, seq, D_MODEL)
```

```python
import functools
import math

import jax
import jax.numpy as jnp
from jax import lax
from jax.experimental import pallas as pl
from jax.experimental.pallas import tpu as pltpu

F32 = jnp.float32
BF16 = jnp.bfloat16

D_MODEL = 1024
CONV_WIDTH = 512
CONV_K = 3
N_HEADS = 8
HEAD_DIM = 64
V_DIM = 2 * HEAD_DIM
N_BUCKETS = 32
MAX_DISTANCE = 128
N_EXPERTS = 32
TOP_K = 4
SWIGLU_ALPHA = 1.702
SWIGLU_LIMIT = 7.0
EPS = 1e-6
LAM_INIT = 0.8 - 0.6 * math.exp(-0.3 * 0)

D_NAT = 3 * CONV_WIDTH + 3 * D_MODEL
COL_K, COL_GA, COL_GB = 0, 1024, 2048
COL_CB, COL_CC, COL_CU = 3072, 3584, 4096
D_TR = 2 * D_MODEL
ROW_QT, ROW_VT = 0, 1024

TM_IN = 1024
TN_IN = 1152
TQ = 512
TM_POST = 512
T_SCAN = 512
T_ROW = 256
ROW_BLOCK = 512
MASK_NEG = -1e30
ONES_ROWS = 16
LANES = 128
DEINT_CHUNK = 512
VMEM_LIMIT = 56 * 1024 * 1024


def _sigmoid(x):
    return 1.0 / (1.0 + jnp.exp(-x))


def _rms(x, g):
    return x * lax.rsqrt(jnp.mean(x * x, axis=-1, keepdims=True) + EPS) * g


def _ada_kernel(c_ref, w_ref, b_ref, lq1_ref, lk1_ref, lq2_ref, lk2_ref, mod_ref, lam_ref):
    c = c_ref[...]
    s = c * _sigmoid(c)
    mod_ref[...] = jnp.dot(s, w_ref[...], precision=lax.Precision.HIGHEST,
                           preferred_element_type=F32) + b_ref[...]
    a1 = jnp.sum(lq1_ref[...] * lk1_ref[...], axis=-1, keepdims=True)
    a2 = jnp.sum(lq2_ref[...] * lk2_ref[...], axis=-1, keepdims=True)
    lam_ref[...] = jnp.broadcast_to(jnp.exp(a1) - jnp.exp(a2) + LAM_INIT, lam_ref.shape)


def _ada(c, w_ada, b_ada, lq1, lk1, lq2, lk2):
    bsz = c.shape[0]
    n_chunks = w_ada.shape[1] // D_MODEL
    vec = pl.BlockSpec((1, HEAD_DIM), lambda j: (0, 0))
    return pl.pallas_call(
        _ada_kernel,
        grid=(n_chunks,),
        in_specs=[pl.BlockSpec((bsz, D_MODEL), lambda j: (0, 0)),
                  pl.BlockSpec((D_MODEL, D_MODEL), lambda j: (0, j)),
                  pl.BlockSpec((1, D_MODEL), lambda j: (0, j)),
                  vec, vec, vec, vec],
        out_specs=[pl.BlockSpec((bsz, D_MODEL), lambda j: (0, j)),
                   pl.BlockSpec((1, 128), lambda j: (0, 0))],
        out_shape=[jax.ShapeDtypeStruct((bsz, n_chunks * D_MODEL), F32),
                   jax.ShapeDtypeStruct((1, 128), F32)],
        name="ada",
    )(c, w_ada, b_ada.reshape(1, -1), lq1.reshape(1, -1), lk1.reshape(1, -1),
      lq2.reshape(1, -1), lk2.reshape(1, -1))


def _inproj_kernel(x_ref, mod_ref, g_ref, w_ref, wt_ref, o_ref, ot_ref, h_scr, ht_scr, *, n_nat):
    j = pl.program_id(2)

    @pl.when(j == 0)
    def _():
        h = _rms(x_ref[0], g_ref[...]) * (1.0 + mod_ref[0, 1:2, :]) + mod_ref[0, 0:1, :]
        h_scr[...] = h.astype(BF16)
        ht_scr[...] = h.T.astype(BF16)

    @pl.when(j < n_nat)
    def _():
        o_ref[0] = jnp.dot(h_scr[...], w_ref[...], preferred_element_type=F32).astype(BF16)

    @pl.when(j == n_nat)
    def _():
        ot_ref[0] = jnp.dot(wt_ref[...], ht_scr[...], preferred_element_type=F32).astype(BF16)


def _inproj(x, mod3, norm1_g, w_nat, w_tr):
    bsz, seq, _ = x.shape
    tm = min(TM_IN, seq)
    n_nat = D_NAT // TN_IN
    nat_j = lambda j: jnp.minimum(j, n_nat - 1)
    return pl.pallas_call(
        functools.partial(_inproj_kernel, n_nat=n_nat),
        grid=(bsz, seq // tm, n_nat + 1),
        in_specs=[pl.BlockSpec((1, tm, D_MODEL), lambda b, i, j: (b, i, 0)),
                  pl.BlockSpec((1, 6, D_MODEL), lambda b, i, j: (b, 0, 0)),
                  pl.BlockSpec((1, D_MODEL), lambda b, i, j: (0, 0)),
                  pl.BlockSpec((D_MODEL, TN_IN), lambda b, i, j: (0, nat_j(j))),
                  pl.BlockSpec((D_TR, D_MODEL), lambda b, i, j: (0, 0))],
        out_specs=[pl.BlockSpec((1, tm, TN_IN), lambda b, i, j: (b, i, nat_j(j))),
                   pl.BlockSpec((1, D_TR, tm), lambda b, i, j: (b, 0, i))],
        out_shape=[jax.ShapeDtypeStruct((bsz, seq, D_NAT), BF16),
                   jax.ShapeDtypeStruct((bsz, D_TR, seq), BF16)],
        scratch_shapes=[pltpu.VMEM((tm, D_MODEL), BF16), pltpu.VMEM((D_MODEL, tm), BF16)],
        compiler_params=pltpu.CompilerParams(
            dimension_semantics=("arbitrary", "arbitrary", "arbitrary"),
            vmem_limit_bytes=VMEM_LIMIT),
        name="inproj",
    )(x, mod3, norm1_g.reshape(1, -1), w_nat, w_tr)


def _attn_kernel(lam_ref, qt_ref, k_ref, vt_ref, bias_ref, g_ref, o_ref,
                 m_scr, acc_scr, vte_scr, s_a, s_b, t_a, t_b, *, tq):
    i = pl.program_id(2)

    @pl.when(i == 0)
    def _():
        vte_scr[0:V_DIM, :] = vt_ref[0]
        vte_scr[V_DIM:, :] = jnp.ones((ONES_ROWS, vte_scr.shape[1]), BF16)

    qt = qt_ref[0] * jnp.asarray(HEAD_DIM ** -0.5, BF16)
    row = lax.broadcasted_iota(jnp.int32, qt.shape, 0)
    zero = jnp.zeros_like(qt)
    qt_maps = (jnp.where(row < HEAD_DIM, qt, zero), jnp.where(row >= HEAD_DIM, qt, zero))

    m_scr[...] = jnp.full(m_scr.shape, MASK_NEG, F32)
    acc_scr[...] = jnp.zeros(acc_scr.shape, F32)
    slots = ((s_a, t_a), (s_b, t_b))

    def produce(j, bias, slot):
        s_ref, t_ref = slots[slot]
        k = k_ref[0, pl.ds(pl.multiple_of(j * tq, tq), tq), :]
        for m in range(2):
            s = jnp.dot(k, qt_maps[m], preferred_element_type=F32)
            if bias is not None:
                s = s + bias
            s_ref[m] = s
            t_ref[m] = jnp.max(s, axis=0, keepdims=True)

    def consume(j, slot):
        s_ref, t_ref = slots[slot]
        vte = vte_scr[:, pl.ds(pl.multiple_of(j * tq, tq), tq)]
        for m in range(2):
            m_prev = m_scr[m]
            m_new = jnp.maximum(m_prev, t_ref[m])
            alpha = jnp.exp(m_prev - m_new)
            p = jnp.exp(s_ref[m] - m_new).astype(BF16)
            acc_scr[m] = alpha * acc_scr[m] + jnp.dot(vte, p, preferred_element_type=F32)
            m_scr[m] = m_new

    diag = lambda: bias_ref[0, 0]
    off1 = lambda: bias_ref[0, 1]

    @pl.when(i == 0)
    def _():
        produce(0, diag(), 0)
        consume(0, 0)

    @pl.when(i == 1)
    def _():
        produce(0, off1(), 0)
        produce(1, diag(), 1)
        consume(0, 0)
        consume(1, 1)

    @pl.when(i >= 2)
    def _():
        produce(0, None, 0)

        def pair(r, carry):
            t = 2 * r
            produce(t + 1, None, 1)
            consume(t, 0)
            produce(t + 2, None, 0)
            consume(t + 1, 1)
            return carry

        lax.fori_loop(0, (i - 2) // 2, pair, 0)

        @pl.when(i % 2 == 0)
        def _():
            produce(i - 1, off1(), 1)
            consume(i - 2, 0)
            produce(i, diag(), 0)
            consume(i - 1, 1)
            consume(i, 0)

        @pl.when(i % 2 == 1)
        def _():
            produce(i - 2, None, 1)
            consume(i - 3, 0)
            produce(i - 1, off1(), 0)
            consume(i - 2, 1)
            produce(i, diag(), 1)
            consume(i - 1, 0)
            consume(i, 1)

    lam = lam_ref[0]
    ot = (acc_scr[0, :V_DIM] / acc_scr[0, V_DIM:V_DIM + 1]
          - lam * (acc_scr[1, :V_DIM] / acc_scr[1, V_DIM:V_DIM + 1]))
    ot = ot * lax.rsqrt(jnp.mean(ot * ot, axis=0, keepdims=True) + EPS) * g_ref[...]
    o_ref[0] = (ot * (1.0 - LAM_INIT)).T.astype(BF16)


def _attention(proj, proj_t, lam, bias_tiles, subln_g):
    bsz, seq, _ = proj.shape
    tq = min(TQ, seq)
    nq = seq // tq
    kern = functools.partial(_attn_kernel, tq=tq)
    score = pltpu.VMEM((2, tq, tq), F32)
    colmax = pltpu.VMEM((2, 1, tq), F32)
    return pl.pallas_call(
        kern,
        grid=(bsz, N_HEADS, nq),
        in_specs=[pl.BlockSpec(memory_space=pltpu.SMEM),
                  pl.BlockSpec((1, V_DIM, tq), lambda b, h, i: (b, ROW_QT // V_DIM + h, i)),
                  pl.BlockSpec((1, seq, V_DIM), lambda b, h, i: (b, 0, COL_K // V_DIM + h)),
                  pl.BlockSpec((1, V_DIM, seq), lambda b, h, i: (b, ROW_VT // V_DIM + h, 0)),
                  pl.BlockSpec((1, 2, tq, tq), lambda b, h, i: (h, 0, 0, 0)),
                  pl.BlockSpec((V_DIM, 1), lambda b, h, i: (0, 0))],
        out_specs=pl.BlockSpec((1, tq, V_DIM), lambda b, h, i: (b, i, h)),
        out_shape=jax.ShapeDtypeStruct((bsz, seq, N_HEADS * V_DIM), BF16),
        scratch_shapes=[pltpu.VMEM((2, 1, tq), F32),
                        pltpu.VMEM((2, V_DIM + ONES_ROWS, tq), F32),
                        pltpu.VMEM((V_DIM + ONES_ROWS, seq), BF16),
                        score, score, colmax, colmax],
        compiler_params=pltpu.CompilerParams(
            dimension_semantics=("arbitrary", "arbitrary", "arbitrary"),
            vmem_limit_bytes=VMEM_LIMIT),
        name="attn",
    )(lam, proj_t, proj, proj_t, bias_tiles, subln_g.reshape(-1, 1))


def _t5_bucket(dist):
    max_exact = N_BUCKETS // 2
    d = jnp.maximum(dist, 1).astype(F32)
    large = max_exact + (jnp.log(d / max_exact) / math.log(MAX_DISTANCE / max_exact)
                         * (N_BUCKETS - max_exact)).astype(jnp.int32)
    large = jnp.minimum(large, N_BUCKETS - 1)
    return jnp.where(dist < max_exact, dist, large)


def _bias_tiles(rel_table, tq):
    assert tq >= MAX_DISTANCE
    r = jnp.arange(tq, dtype=jnp.int32)
    rel0 = r[None, :] - r[:, None]
    rel = jnp.stack([rel0, rel0 + tq])
    table = (rel_table - rel_table[N_BUCKETS - 1:N_BUCKETS, :]).astype(F32)
    bucket = _t5_bucket(jnp.maximum(rel, 0))
    onehot = (bucket[None] == jnp.arange(N_BUCKETS, dtype=jnp.int32)[:, None, None, None]).astype(F32)
    bias = jnp.einsum('kh,kabc->habc', table, onehot, precision=lax.Precision.HIGHEST)
    return jnp.where((rel >= 0)[None], bias, MASK_NEG)


def _post_kernel(att_ref, ga_ref, gb_ref, cb_ref, cc_ref, cu_ref, x_ref, mod_ref,
                 cw_ref, cbias_ref, woc_ref, woa_ref, wmg_ref, g2_ref, wr_ref, br_ref,
                 x1_ref, hp_ref, idx_ref, tw_ref, cnt_ref,
                 ubuf, carry, *, tm, tiles_per_seq):
    i = pl.program_id(0)

    @pl.when(i % tiles_per_seq == 0)
    def _():
        carry[...] = jnp.zeros(carry.shape, F32)

    @pl.when(i == 0)
    def _():
        cnt_ref[...] = jnp.zeros(cnt_ref.shape, F32)

    u = cc_ref[...].astype(F32) * cu_ref[...].astype(F32)
    ubuf[0:8, :] = carry[...]
    ubuf[8:8 + tm, :] = u
    carry[...] = u[tm - 8:tm, :]
    conv = (cbias_ref[...] + ubuf[6:6 + tm, :] * cw_ref[0:1, :]
            + ubuf[7:7 + tm, :] * cw_ref[1:2, :] + u * cw_ref[2:3, :])
    ya = jnp.dot((cb_ref[...].astype(F32) * conv).astype(BF16), woc_ref[...],
                 preferred_element_type=F32)
    yb = jnp.dot(att_ref[...], woa_ref[...], preferred_element_type=F32)
    merged = _sigmoid(ga_ref[...].astype(F32)) * ya + _sigmoid(gb_ref[...].astype(F32)) * yb
    mix = jnp.dot(merged.astype(BF16), wmg_ref[...], preferred_element_type=F32)
    x1 = x_ref[...] + mod_ref[0, 2:3, :] * mix
    x1_ref[...] = x1

    h2 = _rms(x1, g2_ref[...]) * (1.0 + mod_ref[0, 4:5, :]) + mod_ref[0, 3:4, :]
    h_hi = h2.astype(BF16)
    h_hi32 = h_hi.astype(F32)
    h_lo = (h2 - h_hi32).astype(BF16)

    part = jnp.dot(h_hi, wr_ref[...], preferred_element_type=F32)
    logits = (part[:, :N_EXPERTS] + part[:, N_EXPERTS:]
              + jnp.dot(h_lo, wr_ref[:, :N_EXPERTS], preferred_element_type=F32)
              + br_ref[...])

    lane = lax.broadcasted_iota(jnp.int32, logits.shape, 1)
    work = logits
    sel_l, sel_i = [], []
    member = jnp.zeros(logits.shape, F32)
    for _ in range(TOP_K):
        mx = jnp.max(work, axis=-1, keepdims=True)
        ix = jnp.min(jnp.where(work == mx, lane, N_EXPERTS), axis=-1, keepdims=True)
        hit = lane == ix
        member = member + hit.astype(F32)
        work = jnp.where(hit, -jnp.inf, work)
        sel_l.append(mx)
        sel_i.append(ix)
    ex = [jnp.exp(l - sel_l[0]) for l in sel_l]
    den = ex[0] + ex[1] + ex[2] + ex[3]
    tw_ref[...] = jnp.concatenate([e / den for e in ex], axis=1)
    idx_ref[...] = jnp.concatenate(sel_i, axis=1)
    cnt_ref[...] += jnp.sum(member, axis=0, keepdims=True)

    bits = pltpu.bitcast(h_hi32, jnp.uint32)
    half = D_MODEL // 2
    hp_ref[...] = (bits[:, half:] & jnp.uint32(0xFFFF0000)) | (bits[:, :half] >> 16)


def _post(att, proj, x2, mod3, conv_w, conv_b, woc_b, woa_b, wmg_b, norm2_g, wr_split, b_router, seq):
    n = x2.shape[0]
    tm = min(TM_POST, seq)
    tiles_per_seq = seq // tm
    kern = functools.partial(_post_kernel, tm=tm, tiles_per_seq=tiles_per_seq)
    row = lambda c: (lambda i: (i, c))
    const = lambda i: (0, 0)
    return pl.pallas_call(
        kern,
        grid=(n // tm,),
        in_specs=[pl.BlockSpec((tm, D_MODEL), row(0)),
                  pl.BlockSpec((tm, D_MODEL), row(COL_GA // D_MODEL)),
                  pl.BlockSpec((tm, D_MODEL), row(COL_GB // D_MODEL)),
                  pl.BlockSpec((tm, CONV_WIDTH), row(COL_CB // CONV_WIDTH)),
                  pl.BlockSpec((tm, CONV_WIDTH), row(COL_CC // CONV_WIDTH)),
                  pl.BlockSpec((tm, CONV_WIDTH), row(COL_CU // CONV_WIDTH)),
                  pl.BlockSpec((tm, D_MODEL), row(0)),
                  pl.BlockSpec((1, 6, D_MODEL), lambda i: (i // tiles_per_seq, 0, 0)),
                  pl.BlockSpec((CONV_K, CONV_WIDTH), const),
                  pl.BlockSpec((1, CONV_WIDTH), const),
                  pl.BlockSpec((CONV_WIDTH, D_MODEL), const),
                  pl.BlockSpec((D_MODEL, D_MODEL), const),
                  pl.BlockSpec((D_MODEL, D_MODEL), const),
                  pl.BlockSpec((1, D_MODEL), const),
                  pl.BlockSpec((D_MODEL, 2 * N_EXPERTS), const),
                  pl.BlockSpec((1, N_EXPERTS), const)],
        out_specs=[pl.BlockSpec((tm, D_MODEL), row(0)),
                   pl.BlockSpec((tm, D_MODEL // 2), row(0)),
                   pl.BlockSpec((tm, TOP_K), row(0)),
                   pl.BlockSpec((tm, TOP_K), row(0)),
                   pl.BlockSpec((1, N_EXPERTS), const)],
        out_shape=[jax.ShapeDtypeStruct((n, D_MODEL), F32),
                   jax.ShapeDtypeStruct((n, D_MODEL // 2), jnp.uint32),
                   jax.ShapeDtypeStruct((n, TOP_K), jnp.int32),
                   jax.ShapeDtypeStruct((n, TOP_K), F32),
                   jax.ShapeDtypeStruct((1, N_EXPERTS), F32)],
        scratch_shapes=[pltpu.VMEM((tm + 8, CONV_WIDTH), F32), pltpu.VMEM((8, CONV_WIDTH), F32)],
        compiler_params=pltpu.CompilerParams(dimension_semantics=("arbitrary",),
                                             vmem_limit_bytes=VMEM_LIMIT),
        name="post",
    )(att, proj, proj, proj, proj, proj, x2, mod3, conv_w, conv_b.reshape(1, -1),
      woc_b, woa_b, wmg_b, norm2_g.reshape(1, -1), wr_split, b_router.reshape(1, -1))


def _scan_kernel(idx_ref, pstart_ref, dest_ref, carry, *, t):
    @pl.when(pl.program_id(0) == 0)
    def _():
        carry[...] = jnp.zeros(carry.shape, F32)

    idx = idx_ref[...]
    lane = lax.broadcasted_iota(jnp.int32, (t, N_EXPERTS), 1)
    hits = [lane == idx[:, k:k + 1] for k in range(TOP_K)]
    member = hits[0].astype(F32)
    for k in range(1, TOP_K):
        member = member + hits[k].astype(F32)
    r = lax.broadcasted_iota(jnp.int32, (t, t), 0)
    c = lax.broadcasted_iota(jnp.int32, (t, t), 1)
    earlier = jnp.where(c < r, 1.0, 0.0).astype(BF16)
    rank = jnp.dot(earlier, member.astype(BF16), preferred_element_type=F32) + carry[...]
    dest_e = pstart_ref[...] + rank
    cols = [jnp.sum(jnp.where(hits[k], dest_e, 0.0), axis=-1, keepdims=True) for k in range(TOP_K)]
    dest_ref[...] = jnp.concatenate(cols, axis=1).astype(jnp.int32)
    carry[...] += jnp.sum(member, axis=0, keepdims=True)


def _scan(idx, pstart):
    n = idx.shape[0]
    t = min(T_SCAN, n)
    return pl.pallas_call(
        functools.partial(_scan_kernel, t=t),
        grid=(n // t,),
        in_specs=[pl.BlockSpec((t, TOP_K), lambda i: (i, 0)),
                  pl.BlockSpec((1, N_EXPERTS), lambda i: (0, 0))],
        out_specs=pl.BlockSpec((t, TOP_K), lambda i: (i, 0)),
        out_shape=jax.ShapeDtypeStruct((n, TOP_K), jnp.int32),
        scratch_shapes=[pltpu.VMEM((1, N_EXPERTS), F32)],
        compiler_params=pltpu.CompilerParams(dimension_semantics=("arbitrary",)),
        name="scan",
    )(idx, pstart)


def _row_copy(src_ref, src_row, dst_ref, dst_row, sem):
    return pltpu.make_async_copy(src_ref.at[pl.ds(src_row, 1)], dst_ref.at[pl.ds(dst_row, 1)], sem)


def _dispatch_kernel(dest_ref, hp_ref, zeros_ref, xs_ref, sem, *, t):
    del zeros_ref

    def issue(tok, carry):
        for k in range(TOP_K):
            _row_copy(hp_ref, tok, xs_ref, dest_ref[tok * TOP_K + k], sem).start()
        return carry

    lax.fori_loop(0, t, issue, 0)

    def drain(tok, carry):
        for k in range(TOP_K):
            _row_copy(hp_ref, tok, xs_ref, dest_ref[tok * TOP_K + k], sem).wait()
        return carry

    lax.fori_loop(0, t, drain, 0)


def _dispatch(dest_flat, hp, rows):
    n, width = hp.shape
    t = min(T_ROW, n)
    zeros = jnp.zeros((rows, width), hp.dtype)
    return pl.pallas_call(
        functools.partial(_dispatch_kernel, t=t),
        grid=(n // t,),
        in_specs=[pl.BlockSpec((t * TOP_K,), lambda i: (i,), memory_space=pltpu.SMEM),
                  pl.BlockSpec((t, width), lambda i: (i, 0)),
                  pl.BlockSpec(memory_space=pl.ANY)],
        out_specs=pl.BlockSpec(memory_space=pl.ANY),
        out_shape=jax.ShapeDtypeStruct((rows, width), hp.dtype),
        scratch_shapes=[pltpu.SemaphoreType.DMA],
        input_output_aliases={2: 0},
        compiler_params=pltpu.CompilerParams(dimension_semantics=("arbitrary",),
                                             has_side_effects=True),
        name="dispatch",
    )(dest_flat, hp, zeros)


def _expert_kernel(be_ref, first_ref, used_ref, xs_ref, wgu_ref, bg_ref, bl_ref, wd_ref, bd_ref,
                   y_ref, wg_s, wl_s, wd_s, tmp_s):
    del be_ref
    i = pl.program_id(0)

    @pl.when(first_ref[i] == 1)
    def _():
        half = DEINT_CHUNK // 2
        for c in range(wgu_ref.shape[2] // DEINT_CHUNK):
            for db in range(D_MODEL // LANES):
                rows = slice(db * LANES, (db + 1) * LANES)
                tmp_s[...] = wgu_ref[0, rows, c * DEINT_CHUNK:(c + 1) * DEINT_CHUNK].T
                cols = slice(c * half, (c + 1) * half)
                wg_s[rows, cols] = tmp_s[pl.ds(0, half, stride=2), :].T.astype(BF16)
                wl_s[rows, cols] = tmp_s[pl.ds(1, half, stride=2), :].T.astype(BF16)
        wd_s[...] = wd_ref[0].astype(BF16)

    @pl.when(i < used_ref[0])
    def _():
        xp = xs_ref[...]
        lo = pltpu.bitcast(xp << 16, F32)
        hi = pltpu.bitcast(xp & jnp.uint32(0xFFFF0000), F32)
        xb = jnp.concatenate([lo, hi], axis=1).astype(BF16)
        g = jnp.dot(xb, wg_s[...], preferred_element_type=F32) + bg_ref[0]
        l = jnp.dot(xb, wl_s[...], preferred_element_type=F32) + bl_ref[0]
        glu = jnp.minimum(g, SWIGLU_LIMIT)
        lin = jnp.clip(l, -SWIGLU_LIMIT, SWIGLU_LIMIT)
        act = glu * _sigmoid(SWIGLU_ALPHA * glu) * (lin + 1.0)
        y_ref[...] = jnp.dot(act.astype(BF16), wd_s[...], preferred_element_type=F32) + bd_ref[0]

    @pl.when(i >= used_ref[0])
    def _():
        y_ref[...] = jnp.zeros(y_ref.shape, F32)


def _experts(block_e, first, n_used, xs, wgu, bg, bl, wd, bd):
    rows, width = xs.shape
    d_exp = wd.shape[1]
    n_blocks = rows // ROW_BLOCK
    wsel = lambda i, be, fi, nu: (be[i], 0, 0)
    rsel = lambda i, be, fi, nu: (jnp.minimum(i, nu[0] - 1), 0)
    grid_spec = pltpu.PrefetchScalarGridSpec(
        num_scalar_prefetch=3,
        grid=(n_blocks,),
        in_specs=[pl.BlockSpec((ROW_BLOCK, width), rsel),
                  pl.BlockSpec((1, D_MODEL, 2 * d_exp), wsel),
                  pl.BlockSpec((1, 1, d_exp), wsel),
                  pl.BlockSpec((1, 1, d_exp), wsel),
                  pl.BlockSpec((1, d_exp, D_MODEL), wsel),
                  pl.BlockSpec((1, 1, D_MODEL), wsel)],
        out_specs=pl.BlockSpec((ROW_BLOCK, D_MODEL), lambda i, be, fi, nu: (i, 0)),
        scratch_shapes=[pltpu.VMEM((D_MODEL, d_exp), BF16), pltpu.VMEM((D_MODEL, d_exp), BF16),
                        pltpu.VMEM((d_exp, D_MODEL), BF16), pltpu.VMEM((DEINT_CHUNK, LANES), F32)],
    )
    return pl.pallas_call(
        _expert_kernel,
        grid_spec=grid_spec,
        out_shape=jax.ShapeDtypeStruct((rows, D_MODEL), F32),
        compiler_params=pltpu.CompilerParams(dimension_semantics=("arbitrary",),
                                             vmem_limit_bytes=VMEM_LIMIT),
        name="experts",
    )(block_e, first, n_used, xs, wgu, bg, bl, wd, bd)


def _combine_kernel(dest_ref, y_ref, x1_ref, tw_ref, mod_ref, gf_ref, o_ref, buf, sem, *, t):
    def issue(tok, carry):
        for k in range(TOP_K):
            _row_copy(y_ref, dest_ref[tok * TOP_K + k], buf.at[k], tok, sem).start()
        return carry

    lax.fori_loop(0, t, issue, 0)

    def drain(tok, carry):
        for k in range(TOP_K):
            _row_copy(y_ref, dest_ref[tok * TOP_K + k], buf.at[k], tok, sem).wait()
        return carry

    lax.fori_loop(0, t, drain, 0)

    tw = tw_ref[...]
    moe = tw[:, 0:1] * buf[0]
    for k in range(1, TOP_K):
        moe = moe + tw[:, k:k + 1] * buf[k]
    xo = x1_ref[...] + mod_ref[0, 5:6, :] * moe
    o_ref[...] = _rms(xo, gf_ref[...])


def _combine(dest_flat, y, x1, tw, mod3, normf_g, seq):
    n = x1.shape[0]
    t = min(T_ROW, seq)
    tiles_per_seq = seq // t
    return pl.pallas_call(
        functools.partial(_combine_kernel, t=t),
        grid=(n // t,),
        in_specs=[pl.BlockSpec((t * TOP_K,), lambda i: (i,), memory_space=pltpu.SMEM),
                  pl.BlockSpec(memory_space=pl.ANY),
                  pl.BlockSpec((t, D_MODEL), lambda i: (i, 0)),
                  pl.BlockSpec((t, TOP_K), lambda i: (i, 0)),
                  pl.BlockSpec((1, 6, D_MODEL), lambda i: (i // tiles_per_seq, 0, 0)),
                  pl.BlockSpec((1, D_MODEL), lambda i: (0, 0))],
        out_specs=pl.BlockSpec((t, D_MODEL), lambda i: (i, 0)),
        out_shape=jax.ShapeDtypeStruct((n, D_MODEL), F32),
        scratch_shapes=[pltpu.VMEM((TOP_K, t, D_MODEL), F32), pltpu.SemaphoreType.DMA],
        compiler_params=pltpu.CompilerParams(dimension_semantics=("arbitrary",)),
        name="combine",
    )(dest_flat, y, x1, tw, mod3, normf_g.reshape(1, -1))


def _in_proj_weights(w_in):
    c0 = 3 * CONV_WIDTH
    q, k, v = (w_in[:, c0 + s * D_MODEL:c0 + (s + 1) * D_MODEL] for s in range(3))
    w_nat = jnp.concatenate([k, w_in[:, c0 + 3 * D_MODEL:], w_in[:, :c0]], axis=1).astype(BF16)
    w_tr = jnp.concatenate([q, v], axis=1).T.astype(BF16)
    return w_nat, w_tr


def kernel(x, c, w_ada, b_ada, norm1_g, norm2_g, w_in, conv_w, conv_b, w_out_conv, lambda_q1, lambda_k1, lambda_q2, lambda_k2, subln_g, w_o_attn, w_merge, w_router, b_router, w_gate_up, b_gate_up, w_down, b_down, rel_bias_table, normf_g):
    bsz, seq, _ = x.shape
    n = bsz * seq
    assert w_ada.shape[0] == 1, "single layer"
    assert seq % min(TQ, seq) == 0 and n % T_ROW == 0 and (n * TOP_K) % ROW_BLOCK == 0

    mod, lam = _ada(c, w_ada[0], b_ada[0], lambda_q1[0], lambda_k1[0], lambda_q2[0], lambda_k2[0])
    mod3 = mod.reshape(bsz, 6, D_MODEL)

    proj, proj_t = _inproj(x, mod3, norm1_g[0], *_in_proj_weights(w_in[0]))
    att = _attention(proj, proj_t, lam[0, :1], _bias_tiles(rel_bias_table, min(TQ, seq)), subln_g[0])

    wr = w_router[0]
    wr_hi = wr.astype(BF16)
    wr_lo = (wr - wr_hi.astype(F32)).astype(BF16)
    x1, hp, idx, tw, counts = _post(
        att.reshape(n, -1), proj.reshape(n, -1), x.reshape(n, -1), mod3, conv_w[0], conv_b[0],
        w_out_conv[0].astype(BF16), w_o_attn[0].astype(BF16), w_merge[0].astype(BF16),
        norm2_g[0], jnp.concatenate([wr_hi, wr_lo], axis=1), b_router[0], seq)

    cnt = counts[0].astype(jnp.int32)
    padded = (cnt + ROW_BLOCK - 1) // ROW_BLOCK * ROW_BLOCK
    pend = jnp.cumsum(padded)
    pstart = pend - padded
    n_blocks = n * TOP_K // ROW_BLOCK + N_EXPERTS
    rows = n_blocks * ROW_BLOCK
    block_start = jnp.arange(n_blocks, dtype=jnp.int32) * ROW_BLOCK
    block_e = jnp.minimum(jnp.sum(pend[None, :] <= block_start[:, None], axis=1),
                          N_EXPERTS - 1).astype(jnp.int32)

    dest = _scan(idx, pstart.astype(F32).reshape(1, -1)).reshape(-1)
    xs = _dispatch(dest, hp, rows)

    n_used = (pend[-1:] // ROW_BLOCK).astype(jnp.int32)
    changed = jnp.concatenate([jnp.ones((1,), jnp.bool_), block_e[1:] != block_e[:-1]])
    first = (changed & (jnp.arange(n_blocks) < n_used[0])).astype(jnp.int32)
    bgu = b_gate_up[0]
    y = _experts(block_e, first, n_used, xs, w_gate_up[0], bgu[:, None, 0::2], bgu[:, None, 1::2],
                 w_down[0], b_down[0][:, None, :])

    out = _combine(dest, y, x1, tw, mod3, normf_g, seq)
    return out.reshape(bsz, seq, D_MODEL)
```

```python
import functools
import math

import jax
import jax.numpy as jnp
from jax import lax
from jax.experimental import pallas as pl
from jax.experimental.pallas import tpu as pltpu
from jax.experimental.pallas import tpu_sc as plsc

F32 = jnp.float32
BF16 = jnp.bfloat16

D_MODEL = 1024
CONV_WIDTH = 512
CONV_K = 3
N_HEADS = 8
HEAD_DIM = 64
V_DIM = 2 * HEAD_DIM
N_BUCKETS = 32
MAX_DISTANCE = 128
N_EXPERTS = 32
TOP_K = 4
SWIGLU_ALPHA = 1.702
SWIGLU_LIMIT = 7.0
EPS = 1e-6
LAM_INIT = 0.8 - 0.6 * math.exp(-0.3 * 0)

D_NAT = 3 * CONV_WIDTH + 3 * D_MODEL
COL_K, COL_GA, COL_GB = 0, 1024, 2048
COL_CB, COL_CC, COL_CU = 3072, 3584, 4096
D_TR = 2 * D_MODEL
ROW_QT, ROW_VT = 0, 1024

TM_IN = 1024
TN_IN = 1152
TQ = 512
TM_POST = 512
T_SCAN = 512
SC_WINDOW = 128
ROW_BLOCK = 512
MASK_NEG = -1e30
ONES_ROWS = 16
LANES = 128
DEINT_CHUNK = 512
VMEM_LIMIT = 56 * 1024 * 1024


def _sigmoid(x):
    return 1.0 / (1.0 + jnp.exp(-x))


def _rms(x, g):
    return x * lax.rsqrt(jnp.mean(x * x, axis=-1, keepdims=True) + EPS) * g


def _pack_halves(v):
    bits = pltpu.bitcast(v.astype(BF16).astype(F32), jnp.uint32)
    w = v.shape[1] // 2
    return (bits[:, w:] & jnp.uint32(0xFFFF0000)) | (bits[:, :w] >> 16)


def _unpack_halves(words):
    return (pltpu.bitcast(words << 16, F32), pltpu.bitcast(words & jnp.uint32(0xFFFF0000), F32))


def _ada_kernel(c_ref, w_ref, b_ref, lq1_ref, lk1_ref, lq2_ref, lk2_ref, mod_ref, lam_ref):
    c = c_ref[...]
    s = c * _sigmoid(c)
    mod_ref[...] = jnp.dot(s, w_ref[...], precision=lax.Precision.HIGHEST,
                           preferred_element_type=F32) + b_ref[...]
    a1 = jnp.sum(lq1_ref[...] * lk1_ref[...], axis=-1, keepdims=True)
    a2 = jnp.sum(lq2_ref[...] * lk2_ref[...], axis=-1, keepdims=True)
    lam_ref[...] = jnp.broadcast_to(jnp.exp(a1) - jnp.exp(a2) + LAM_INIT, lam_ref.shape)


def _ada(c, w_ada, b_ada, lq1, lk1, lq2, lk2):
    bsz = c.shape[0]
    n_chunks = w_ada.shape[1] // D_MODEL
    vec = pl.BlockSpec((1, HEAD_DIM), lambda j: (0, 0))
    return pl.pallas_call(
        _ada_kernel,
        grid=(n_chunks,),
        in_specs=[pl.BlockSpec((bsz, D_MODEL), lambda j: (0, 0)),
                  pl.BlockSpec((D_MODEL, D_MODEL), lambda j: (0, j)),
                  pl.BlockSpec((1, D_MODEL), lambda j: (0, j)),
                  vec, vec, vec, vec],
        out_specs=[pl.BlockSpec((bsz, D_MODEL), lambda j: (0, j)),
                   pl.BlockSpec((1, 128), lambda j: (0, 0))],
        out_shape=[jax.ShapeDtypeStruct((bsz, n_chunks * D_MODEL), F32),
                   jax.ShapeDtypeStruct((1, 128), F32)],
        name="ada",
    )(c, w_ada, b_ada.reshape(1, -1), lq1.reshape(1, -1), lk1.reshape(1, -1),
      lq2.reshape(1, -1), lk2.reshape(1, -1))


def _inproj_kernel(x_ref, mod_ref, g_ref, w_ref, wt_ref, o_ref, ot_ref, h_scr, ht_scr, *, n_nat):
    j = pl.program_id(2)

    @pl.when(j == 0)
    def _():
        h = _rms(x_ref[0], g_ref[...]) * (1.0 + mod_ref[0, 1:2, :]) + mod_ref[0, 0:1, :]
        h_scr[...] = h.astype(BF16)
        ht_scr[...] = h.T.astype(BF16)

    @pl.when(j < n_nat)
    def _():
        o_ref[0] = jnp.dot(h_scr[...], w_ref[...], preferred_element_type=F32).astype(BF16)

    @pl.when(j == n_nat)
    def _():
        ot_ref[0] = jnp.dot(wt_ref[...], ht_scr[...], preferred_element_type=F32).astype(BF16)


def _inproj(x, mod3, norm1_g, w_nat, w_tr):
    bsz, seq, _ = x.shape
    tm = min(TM_IN, seq)
    n_nat = D_NAT // TN_IN
    nat_j = lambda j: jnp.minimum(j, n_nat - 1)
    return pl.pallas_call(
        functools.partial(_inproj_kernel, n_nat=n_nat),
        grid=(bsz, seq // tm, n_nat + 1),
        in_specs=[pl.BlockSpec((1, tm, D_MODEL), lambda b, i, j: (b, i, 0)),
                  pl.BlockSpec((1, 6, D_MODEL), lambda b, i, j: (b, 0, 0)),
                  pl.BlockSpec((1, D_MODEL), lambda b, i, j: (0, 0)),
                  pl.BlockSpec((D_MODEL, TN_IN), lambda b, i, j: (0, nat_j(j))),
                  pl.BlockSpec((D_TR, D_MODEL), lambda b, i, j: (0, 0))],
        out_specs=[pl.BlockSpec((1, tm, TN_IN), lambda b, i, j: (b, i, nat_j(j))),
                   pl.BlockSpec((1, D_TR, tm), lambda b, i, j: (b, 0, i))],
        out_shape=[jax.ShapeDtypeStruct((bsz, seq, D_NAT), BF16),
                   jax.ShapeDtypeStruct((bsz, D_TR, seq), BF16)],
        scratch_shapes=[pltpu.VMEM((tm, D_MODEL), BF16), pltpu.VMEM((D_MODEL, tm), BF16)],
        compiler_params=pltpu.CompilerParams(
            dimension_semantics=("arbitrary", "arbitrary", "arbitrary"),
            vmem_limit_bytes=VMEM_LIMIT),
        name="inproj",
    )(x, mod3, norm1_g.reshape(1, -1), w_nat, w_tr)


def _attn_kernel(lam_ref, qt_ref, k_ref, vt_ref, bias_ref, g_ref, o_ref,
                 m_scr, acc_scr, vte_scr, s_a, s_b, t_a, t_b, *, tq):
    i = pl.program_id(2)

    @pl.when(i == 0)
    def _():
        vte_scr[0:V_DIM, :] = vt_ref[0]
        vte_scr[V_DIM:, :] = jnp.ones((ONES_ROWS, vte_scr.shape[1]), BF16)

    qt = qt_ref[0] * jnp.asarray(HEAD_DIM ** -0.5, BF16)
    row = lax.broadcasted_iota(jnp.int32, qt.shape, 0)
    zero = jnp.zeros_like(qt)
    qt_maps = (jnp.where(row < HEAD_DIM, qt, zero), jnp.where(row >= HEAD_DIM, qt, zero))

    m_scr[...] = jnp.full(m_scr.shape, MASK_NEG, F32)
    acc_scr[...] = jnp.zeros(acc_scr.shape, F32)
    slots = ((s_a, t_a), (s_b, t_b))

    def produce(j, bias, slot):
        s_ref, t_ref = slots[slot]
        k = k_ref[0, pl.ds(pl.multiple_of(j * tq, tq), tq), :]
        for m in range(2):
            s = jnp.dot(k, qt_maps[m], preferred_element_type=F32)
            if bias is not None:
                s = s + bias
            s_ref[m] = s
            t_ref[m] = jnp.max(s, axis=0, keepdims=True)

    def consume(j, slot):
        s_ref, t_ref = slots[slot]
        vte = vte_scr[:, pl.ds(pl.multiple_of(j * tq, tq), tq)]
        for m in range(2):
            m_prev = m_scr[m]
            m_new = jnp.maximum(m_prev, t_ref[m])
            alpha = jnp.exp(m_prev - m_new)
            p = jnp.exp(s_ref[m] - m_new).astype(BF16)
            acc_scr[m] = alpha * acc_scr[m] + jnp.dot(vte, p, preferred_element_type=F32)
            m_scr[m] = m_new

    diag = lambda: bias_ref[0, 0]
    off1 = lambda: bias_ref[0, 1]

    @pl.when(i == 0)
    def _():
        produce(0, diag(), 0)
        consume(0, 0)

    @pl.when(i == 1)
    def _():
        produce(0, off1(), 0)
        produce(1, diag(), 1)
        consume(0, 0)
        consume(1, 1)

    @pl.when(i >= 2)
    def _():
        produce(0, None, 0)

        def pair(r, carry):
            t = 2 * r
            produce(t + 1, None, 1)
            consume(t, 0)
            produce(t + 2, None, 0)
            consume(t + 1, 1)
            return carry

        lax.fori_loop(0, (i - 2) // 2, pair, 0)

        @pl.when(i % 2 == 0)
        def _():
            produce(i - 1, off1(), 1)
            consume(i - 2, 0)
            produce(i, diag(), 0)
            consume(i - 1, 1)
            consume(i, 0)

        @pl.when(i % 2 == 1)
        def _():
            produce(i - 2, None, 1)
            consume(i - 3, 0)
            produce(i - 1, off1(), 0)
            consume(i - 2, 1)
            produce(i, diag(), 1)
            consume(i - 1, 0)
            consume(i, 1)

    lam = lam_ref[0]
    ot = (acc_scr[0, :V_DIM] / acc_scr[0, V_DIM:V_DIM + 1]
          - lam * (acc_scr[1, :V_DIM] / acc_scr[1, V_DIM:V_DIM + 1]))
    ot = ot * lax.rsqrt(jnp.mean(ot * ot, axis=0, keepdims=True) + EPS) * g_ref[...]
    o_ref[0] = (ot * (1.0 - LAM_INIT)).T.astype(BF16)


def _attention(proj, proj_t, lam, bias_tiles, subln_g):
    bsz, seq, _ = proj.shape
    tq = min(TQ, seq)
    nq = seq // tq
    kern = functools.partial(_attn_kernel, tq=tq)
    score = pltpu.VMEM((2, tq, tq), F32)
    colmax = pltpu.VMEM((2, 1, tq), F32)
    return pl.pallas_call(
        kern,
        grid=(bsz, N_HEADS, nq),
        in_specs=[pl.BlockSpec(memory_space=pltpu.SMEM),
                  pl.BlockSpec((1, V_DIM, tq), lambda b, h, i: (b, ROW_QT // V_DIM + h, i)),
                  pl.BlockSpec((1, seq, V_DIM), lambda b, h, i: (b, 0, COL_K // V_DIM + h)),
                  pl.BlockSpec((1, V_DIM, seq), lambda b, h, i: (b, ROW_VT // V_DIM + h, 0)),
                  pl.BlockSpec((1, 2, tq, tq), lambda b, h, i: (h, 0, 0, 0)),
                  pl.BlockSpec((V_DIM, 1), lambda b, h, i: (0, 0))],
        out_specs=pl.BlockSpec((1, tq, V_DIM), lambda b, h, i: (b, i, h)),
        out_shape=jax.ShapeDtypeStruct((bsz, seq, N_HEADS * V_DIM), BF16),
        scratch_shapes=[pltpu.VMEM((2, 1, tq), F32),
                        pltpu.VMEM((2, V_DIM + ONES_ROWS, tq), F32),
                        pltpu.VMEM((V_DIM + ONES_ROWS, seq), BF16),
                        score, score, colmax, colmax],
        compiler_params=pltpu.CompilerParams(
            dimension_semantics=("arbitrary", "arbitrary", "arbitrary"),
            vmem_limit_bytes=VMEM_LIMIT),
        name="attn",
    )(lam, proj_t, proj, proj_t, bias_tiles, subln_g.reshape(-1, 1))


def _t5_bucket(dist):
    max_exact = N_BUCKETS // 2
    d = jnp.maximum(dist, 1).astype(F32)
    large = max_exact + (jnp.log(d / max_exact) / math.log(MAX_DISTANCE / max_exact)
                         * (N_BUCKETS - max_exact)).astype(jnp.int32)
    large = jnp.minimum(large, N_BUCKETS - 1)
    return jnp.where(dist < max_exact, dist, large)


def _bias_tiles(rel_table, tq):
    assert tq >= MAX_DISTANCE
    r = jnp.arange(tq, dtype=jnp.int32)
    rel0 = r[None, :] - r[:, None]
    rel = jnp.stack([rel0, rel0 + tq])
    table = (rel_table - rel_table[N_BUCKETS - 1:N_BUCKETS, :]).astype(F32)
    bucket = _t5_bucket(jnp.maximum(rel, 0))
    onehot = (bucket[None] == jnp.arange(N_BUCKETS, dtype=jnp.int32)[:, None, None, None]).astype(F32)
    bias = jnp.einsum('kh,kabc->habc', table, onehot, precision=lax.Precision.HIGHEST)
    return jnp.where((rel >= 0)[None], bias, MASK_NEG)


def _post_kernel(att_ref, ga_ref, gb_ref, cb_ref, cc_ref, cu_ref, x_ref, mod_ref,
                 cw_ref, cbias_ref, woc_ref, woa_ref, wmg_ref, g2_ref, wr_ref, br_ref,
                 x1_ref, hp_ref, idx_ref, tw_ref, cnt_ref,
                 ubuf, carry, *, tm, tiles_per_seq):
    i = pl.program_id(0)

    @pl.when(i % tiles_per_seq == 0)
    def _():
        carry[...] = jnp.zeros(carry.shape, F32)

    @pl.when(i == 0)
    def _():
        cnt_ref[...] = jnp.zeros(cnt_ref.shape, F32)

    u = cc_ref[...].astype(F32) * cu_ref[...].astype(F32)
    ubuf[0:8, :] = carry[...]
    ubuf[8:8 + tm, :] = u
    carry[...] = u[tm - 8:tm, :]
    conv = (cbias_ref[...] + ubuf[6:6 + tm, :] * cw_ref[0:1, :]
            + ubuf[7:7 + tm, :] * cw_ref[1:2, :] + u * cw_ref[2:3, :])
    ya = jnp.dot((cb_ref[...].astype(F32) * conv).astype(BF16), woc_ref[...],
                 preferred_element_type=F32)
    yb = jnp.dot(att_ref[...], woa_ref[...], preferred_element_type=F32)
    merged = _sigmoid(ga_ref[...].astype(F32)) * ya + _sigmoid(gb_ref[...].astype(F32)) * yb
    mix = jnp.dot(merged.astype(BF16), wmg_ref[...], preferred_element_type=F32)
    x1 = x_ref[...] + mod_ref[0, 2:3, :] * mix
    x1_ref[...] = x1

    h2 = _rms(x1, g2_ref[...]) * (1.0 + mod_ref[0, 4:5, :]) + mod_ref[0, 3:4, :]
    h_hi = h2.astype(BF16)
    h_hi32 = h_hi.astype(F32)
    h_lo = (h2 - h_hi32).astype(BF16)

    part = jnp.dot(h_hi, wr_ref[...], preferred_element_type=F32)
    logits = (part[:, :N_EXPERTS] + part[:, N_EXPERTS:]
              + jnp.dot(h_lo, wr_ref[:, :N_EXPERTS], preferred_element_type=F32)
              + br_ref[...])

    lane = lax.broadcasted_iota(jnp.int32, logits.shape, 1)
    work = logits
    sel_l, sel_i = [], []
    member = jnp.zeros(logits.shape, F32)
    for _ in range(TOP_K):
        mx = jnp.max(work, axis=-1, keepdims=True)
        ix = jnp.min(jnp.where(work == mx, lane, N_EXPERTS), axis=-1, keepdims=True)
        hit = lane == ix
        member = member + hit.astype(F32)
        work = jnp.where(hit, -jnp.inf, work)
        sel_l.append(mx)
        sel_i.append(ix)
    ex = [jnp.exp(l - sel_l[0]) for l in sel_l]
    den = ex[0] + ex[1] + ex[2] + ex[3]
    tw_ref[...] = jnp.concatenate([e / den for e in ex], axis=1)
    idx_ref[...] = jnp.concatenate(sel_i, axis=1)
    cnt_ref[...] += jnp.sum(member, axis=0, keepdims=True)

    hp_ref[...] = _pack_halves(h2)


def _post(att, proj, x2, mod3, conv_w, conv_b, woc_b, woa_b, wmg_b, norm2_g, wr_split, b_router, seq):
    n = x2.shape[0]
    tm = min(TM_POST, seq)
    tiles_per_seq = seq // tm
    kern = functools.partial(_post_kernel, tm=tm, tiles_per_seq=tiles_per_seq)
    row = lambda c: (lambda i: (i, c))
    const = lambda i: (0, 0)
    return pl.pallas_call(
        kern,
        grid=(n // tm,),
        in_specs=[pl.BlockSpec((tm, D_MODEL), row(0)),
                  pl.BlockSpec((tm, D_MODEL), row(COL_GA // D_MODEL)),
                  pl.BlockSpec((tm, D_MODEL), row(COL_GB // D_MODEL)),
                  pl.BlockSpec((tm, CONV_WIDTH), row(COL_CB // CONV_WIDTH)),
                  pl.BlockSpec((tm, CONV_WIDTH), row(COL_CC // CONV_WIDTH)),
                  pl.BlockSpec((tm, CONV_WIDTH), row(COL_CU // CONV_WIDTH)),
                  pl.BlockSpec((tm, D_MODEL), row(0)),
                  pl.BlockSpec((1, 6, D_MODEL), lambda i: (i // tiles_per_seq, 0, 0)),
                  pl.BlockSpec((CONV_K, CONV_WIDTH), const),
                  pl.BlockSpec((1, CONV_WIDTH), const),
                  pl.BlockSpec((CONV_WIDTH, D_MODEL), const),
                  pl.BlockSpec((D_MODEL, D_MODEL), const),
                  pl.BlockSpec((D_MODEL, D_MODEL), const),
                  pl.BlockSpec((1, D_MODEL), const),
                  pl.BlockSpec((D_MODEL, 2 * N_EXPERTS), const),
                  pl.BlockSpec((1, N_EXPERTS), const)],
        out_specs=[pl.BlockSpec((tm, D_MODEL), row(0)),
                   pl.BlockSpec((tm, D_MODEL // 2), row(0)),
                   pl.BlockSpec((tm, TOP_K), row(0)),
                   pl.BlockSpec((tm, TOP_K), row(0)),
                   pl.BlockSpec((1, N_EXPERTS), const)],
        out_shape=[jax.ShapeDtypeStruct((n, D_MODEL), F32),
                   jax.ShapeDtypeStruct((n, D_MODEL // 2), jnp.uint32),
                   jax.ShapeDtypeStruct((n, TOP_K), jnp.int32),
                   jax.ShapeDtypeStruct((n, TOP_K), F32),
                   jax.ShapeDtypeStruct((1, N_EXPERTS), F32)],
        scratch_shapes=[pltpu.VMEM((tm + 8, CONV_WIDTH), F32), pltpu.VMEM((8, CONV_WIDTH), F32)],
        compiler_params=pltpu.CompilerParams(dimension_semantics=("arbitrary",),
                                             vmem_limit_bytes=VMEM_LIMIT),
        name="post",
    )(att, proj, proj, proj, proj, proj, x2, mod3, conv_w, conv_b.reshape(1, -1),
      woc_b, woa_b, wmg_b, norm2_g.reshape(1, -1), wr_split, b_router.reshape(1, -1))


def _scan_kernel(idx_ref, pstart_ref, dest_ref, carry, *, t):
    @pl.when(pl.program_id(0) == 0)
    def _():
        carry[...] = jnp.zeros(carry.shape, F32)

    idx = idx_ref[...]
    lane = lax.broadcasted_iota(jnp.int32, (t, N_EXPERTS), 1)
    hits = [lane == idx[:, k:k + 1] for k in range(TOP_K)]
    member = hits[0].astype(F32)
    for k in range(1, TOP_K):
        member = member + hits[k].astype(F32)
    r = lax.broadcasted_iota(jnp.int32, (t, t), 0)
    c = lax.broadcasted_iota(jnp.int32, (t, t), 1)
    earlier = jnp.where(c < r, 1.0, 0.0).astype(BF16)
    rank = jnp.dot(earlier, member.astype(BF16), preferred_element_type=F32) + carry[...]
    dest_e = pstart_ref[...] + rank
    cols = [jnp.sum(jnp.where(hits[k], dest_e, 0.0), axis=-1, keepdims=True) for k in range(TOP_K)]
    dest_ref[...] = jnp.concatenate(cols, axis=1).astype(jnp.int32)
    carry[...] += jnp.sum(member, axis=0, keepdims=True)


def _scan(idx, pstart):
    n = idx.shape[0]
    t = min(T_SCAN, n)
    return pl.pallas_call(
        functools.partial(_scan_kernel, t=t),
        grid=(n // t,),
        in_specs=[pl.BlockSpec((t, TOP_K), lambda i: (i, 0)),
                  pl.BlockSpec((1, N_EXPERTS), lambda i: (0, 0))],
        out_specs=pl.BlockSpec((t, TOP_K), lambda i: (i, 0)),
        out_shape=jax.ShapeDtypeStruct((n, TOP_K), jnp.int32),
        scratch_shapes=[pltpu.VMEM((1, N_EXPERTS), F32)],
        compiler_params=pltpu.CompilerParams(dimension_semantics=("arbitrary",)),
        name="scan",
    )(idx, pstart)


def _sc_mesh():
    return plsc.VectorSubcoreMesh(core_axis_name="core", subcore_axis_name="subcore")


def _sc_worker(mesh):
    return lax.axis_index("core") * mesh.num_subcores + lax.axis_index("subcore")


def _sc_dispatch(dest_t, hp, rows):
    n, width = hp.shape
    mesh = _sc_mesh()
    workers = mesh.num_cores * mesh.num_subcores
    per_worker = n // (SC_WINDOW * workers)
    assert per_worker * SC_WINDOW * workers == n

    @pl.kernel(out_type=jax.ShapeDtypeStruct((rows, width), hp.dtype), mesh=mesh,
               scratch_types=[pltpu.VMEM((SC_WINDOW, width), hp.dtype),
                              pltpu.VMEM((TOP_K, SC_WINDOW), jnp.int32)])
    def scatter(hp_hbm, dest_hbm, xs_hbm, row_buf, idx_buf):
        first = _sc_worker(mesh) * per_worker

        @pl.loop(0, per_worker)
        def _(w):
            t0 = (first + w) * SC_WINDOW
            pltpu.sync_copy(hp_hbm.at[pl.ds(t0, SC_WINDOW)], row_buf)
            pltpu.sync_copy(dest_hbm.at[:, pl.ds(t0, SC_WINDOW)], idx_buf)
            for k in range(TOP_K):
                pltpu.sync_copy(row_buf, xs_hbm.at[idx_buf.at[k]])

    return scatter(hp, dest_t)


def _sc_gather(src_rows, table):
    n_out = src_rows.shape[1]
    width = table.shape[1]
    mesh = _sc_mesh()
    workers = mesh.num_cores * mesh.num_subcores
    per_worker = n_out // (SC_WINDOW * workers)
    assert per_worker * SC_WINDOW * workers == n_out

    @pl.kernel(out_type=jax.ShapeDtypeStruct((n_out, width), table.dtype), mesh=mesh,
               scratch_types=[pltpu.VMEM((SC_WINDOW, width), table.dtype),
                              pltpu.VMEM((1, SC_WINDOW), jnp.int32)])
    def gather(table_hbm, src_hbm, out_hbm, row_buf, idx_buf):
        first = _sc_worker(mesh) * per_worker

        @pl.loop(0, per_worker)
        def _(w):
            r0 = (first + w) * SC_WINDOW
            pltpu.sync_copy(src_hbm.at[:, pl.ds(r0, SC_WINDOW)], idx_buf)
            pltpu.sync_copy(table_hbm.at[idx_buf.at[0]], row_buf)
            pltpu.sync_copy(row_buf, out_hbm.at[pl.ds(r0, SC_WINDOW)])

    return gather(table, src_rows)


def _expert_kernel(be_ref, first_ref, used_ref, valid_ref, xs_ref, wgu_ref, bg_ref, bl_ref, wd_ref,
                   bd_ref, y_ref, wg_s, wl_s, wd_s, tmp_s):
    del be_ref
    i = pl.program_id(0)

    @pl.when(first_ref[i] == 1)
    def _():
        half = DEINT_CHUNK // 2
        for c in range(wgu_ref.shape[2] // DEINT_CHUNK):
            for db in range(D_MODEL // LANES):
                rows = slice(db * LANES, (db + 1) * LANES)
                tmp_s[...] = wgu_ref[0, rows, c * DEINT_CHUNK:(c + 1) * DEINT_CHUNK].T
                cols = slice(c * half, (c + 1) * half)
                wg_s[rows, cols] = tmp_s[pl.ds(0, half, stride=2), :].T.astype(BF16)
                wl_s[rows, cols] = tmp_s[pl.ds(1, half, stride=2), :].T.astype(BF16)
        wd_s[...] = wd_ref[0].astype(BF16)

    @pl.when(i < used_ref[0])
    def _():
        xp = xs_ref[...]
        rid = lax.broadcasted_iota(jnp.int32, xp.shape, 0)
        xp = jnp.where(rid < valid_ref[i], xp, jnp.uint32(0))
        xb = jnp.concatenate(_unpack_halves(xp), axis=1).astype(BF16)
        g = jnp.dot(xb, wg_s[...], preferred_element_type=F32) + bg_ref[0]
        l = jnp.dot(xb, wl_s[...], preferred_element_type=F32) + bl_ref[0]
        glu = jnp.minimum(g, SWIGLU_LIMIT)
        lin = jnp.clip(l, -SWIGLU_LIMIT, SWIGLU_LIMIT)
        act = glu * _sigmoid(SWIGLU_ALPHA * glu) * (lin + 1.0)
        y = jnp.dot(act.astype(BF16), wd_s[...], preferred_element_type=F32) + bd_ref[0]
        y_ref[...] = _pack_halves(y)

    @pl.when(i >= used_ref[0])
    def _():
        y_ref[...] = jnp.zeros(y_ref.shape, y_ref.dtype)


def _experts(block_e, first, n_used, valid, xs, wgu, bg, bl, wd, bd):
    rows, width = xs.shape
    d_exp = wd.shape[1]
    n_blocks = rows // ROW_BLOCK
    wsel = lambda i, be, fi, nu, va: (be[i], 0, 0)
    rsel = lambda i, be, fi, nu, va: (jnp.minimum(i, nu[0] - 1), 0)
    grid_spec = pltpu.PrefetchScalarGridSpec(
        num_scalar_prefetch=4,
        grid=(n_blocks,),
        in_specs=[pl.BlockSpec((ROW_BLOCK, width), rsel),
                  pl.BlockSpec((1, D_MODEL, 2 * d_exp), wsel),
                  pl.BlockSpec((1, 1, d_exp), wsel),
                  pl.BlockSpec((1, 1, d_exp), wsel),
                  pl.BlockSpec((1, d_exp, D_MODEL), wsel),
                  pl.BlockSpec((1, 1, D_MODEL), wsel)],
        out_specs=pl.BlockSpec((ROW_BLOCK, width), lambda i, be, fi, nu, va: (i, 0)),
        scratch_shapes=[pltpu.VMEM((D_MODEL, d_exp), BF16), pltpu.VMEM((D_MODEL, d_exp), BF16),
                        pltpu.VMEM((d_exp, D_MODEL), BF16), pltpu.VMEM((DEINT_CHUNK, LANES), F32)],
    )
    return pl.pallas_call(
        _expert_kernel,
        grid_spec=grid_spec,
        out_shape=jax.ShapeDtypeStruct((rows, width), xs.dtype),
        compiler_params=pltpu.CompilerParams(dimension_semantics=("arbitrary",),
                                             vmem_limit_bytes=VMEM_LIMIT),
        name="experts",
    )(block_e, first, n_used, valid, xs, wgu, bg, bl, wd, bd)


def _combine_kernel(yk_ref, x1_ref, tw_ref, mod_ref, gf_ref, o_ref):
    tw = tw_ref[...]
    width = yk_ref.shape[1] // TOP_K
    lo = hi = None
    for k in range(TOP_K):
        y_lo, y_hi = _unpack_halves(yk_ref[:, k * width:(k + 1) * width])
        w = tw[:, k:k + 1]
        lo = w * y_lo if lo is None else lo + w * y_lo
        hi = w * y_hi if hi is None else hi + w * y_hi
    moe = jnp.concatenate([lo, hi], axis=1)
    xo = x1_ref[...] + mod_ref[0, 5:6, :] * moe
    o_ref[...] = _rms(xo, gf_ref[...])


def _combine(yk, x1, tw, mod3, normf_g, seq):
    n = x1.shape[0]
    t = min(TM_POST, seq)
    tiles_per_seq = seq // t
    return pl.pallas_call(
        _combine_kernel,
        grid=(n // t,),
        in_specs=[pl.BlockSpec((t, yk.shape[1]), lambda i: (i, 0)),
                  pl.BlockSpec((t, D_MODEL), lambda i: (i, 0)),
                  pl.BlockSpec((t, TOP_K), lambda i: (i, 0)),
                  pl.BlockSpec((1, 6, D_MODEL), lambda i: (i // tiles_per_seq, 0, 0)),
                  pl.BlockSpec((1, D_MODEL), lambda i: (0, 0))],
        out_specs=pl.BlockSpec((t, D_MODEL), lambda i: (i, 0)),
        out_shape=jax.ShapeDtypeStruct((n, D_MODEL), F32),
        compiler_params=pltpu.CompilerParams(dimension_semantics=("arbitrary",)),
        name="combine",
    )(yk, x1, tw, mod3, normf_g.reshape(1, -1))


def _in_proj_weights(w_in):
    c0 = 3 * CONV_WIDTH
    q, k, v = (w_in[:, c0 + s * D_MODEL:c0 + (s + 1) * D_MODEL] for s in range(3))
    w_nat = jnp.concatenate([k, w_in[:, c0 + 3 * D_MODEL:], w_in[:, :c0]], axis=1).astype(BF16)
    w_tr = jnp.concatenate([q, v], axis=1).T.astype(BF16)
    return w_nat, w_tr


def kernel(x, c, w_ada, b_ada, norm1_g, norm2_g, w_in, conv_w, conv_b, w_out_conv, lambda_q1, lambda_k1, lambda_q2, lambda_k2, subln_g, w_o_attn, w_merge, w_router, b_router, w_gate_up, b_gate_up, w_down, b_down, rel_bias_table, normf_g):
    bsz, seq, _ = x.shape
    n = bsz * seq
    assert w_ada.shape[0] == 1, "single layer"
    assert seq % min(TQ, seq) == 0 and n % TM_POST == 0 and (n * TOP_K) % ROW_BLOCK == 0

    mod, lam = _ada(c, w_ada[0], b_ada[0], lambda_q1[0], lambda_k1[0], lambda_q2[0], lambda_k2[0])
    mod3 = mod.reshape(bsz, 6, D_MODEL)

    proj, proj_t = _inproj(x, mod3, norm1_g[0], *_in_proj_weights(w_in[0]))
    att = _attention(proj, proj_t, lam[0, :1], _bias_tiles(rel_bias_table, min(TQ, seq)), subln_g[0])

    wr = w_router[0]
    wr_hi = wr.astype(BF16)
    wr_lo = (wr - wr_hi.astype(F32)).astype(BF16)
    x1, hp, idx, tw, counts = _post(
        att.reshape(n, -1), proj.reshape(n, -1), x.reshape(n, -1), mod3, conv_w[0], conv_b[0],
        w_out_conv[0].astype(BF16), w_o_attn[0].astype(BF16), w_merge[0].astype(BF16),
        norm2_g[0], jnp.concatenate([wr_hi, wr_lo], axis=1), b_router[0], seq)

    cnt = counts[0].astype(jnp.int32)
    padded = (cnt + ROW_BLOCK - 1) // ROW_BLOCK * ROW_BLOCK
    pend = jnp.cumsum(padded)
    pstart = pend - padded
    n_blocks = n * TOP_K // ROW_BLOCK + N_EXPERTS
    rows = n_blocks * ROW_BLOCK
    block_start = jnp.arange(n_blocks, dtype=jnp.int32) * ROW_BLOCK
    block_e = jnp.minimum(jnp.sum(pend[None, :] <= block_start[:, None], axis=1),
                          N_EXPERTS - 1).astype(jnp.int32)

    dest = _scan(idx, pstart.astype(F32).reshape(1, -1))
    xs = _sc_dispatch(dest.T, hp, rows)

    n_used = (pend[-1:] // ROW_BLOCK).astype(jnp.int32)
    changed = jnp.concatenate([jnp.ones((1,), jnp.bool_), block_e[1:] != block_e[:-1]])
    first = (changed & (jnp.arange(n_blocks) < n_used[0])).astype(jnp.int32)
    valid = jnp.clip((pstart + cnt)[block_e] - block_start, 0, ROW_BLOCK).astype(jnp.int32)
    bgu = b_gate_up[0]
    y = _experts(block_e, first, n_used, valid, xs, w_gate_up[0], bgu[:, None, 0::2],
                 bgu[:, None, 1::2], w_down[0], b_down[0][:, None, :])

    yk = _sc_gather(dest.reshape(1, -1), y).reshape(n, -1)
    out = _combine(yk, x1, tw, mod3, normf_g, seq)
    return out.reshape(bsz, seq, D_MODEL)
```

```python
import functools
import math

import jax
import jax.numpy as jnp
from jax import lax
from jax.experimental import pallas as pl
from jax.experimental.pallas import tpu as pltpu
from jax.experimental.pallas import tpu_sc as plsc

F32 = jnp.float32
BF16 = jnp.bfloat16

D_MODEL = 1024
CONV_WIDTH = 512
CONV_K = 3
N_HEADS = 8
HEAD_DIM = 64
V_DIM = 2 * HEAD_DIM
N_BUCKETS = 32
MAX_DISTANCE = 128
N_EXPERTS = 32
TOP_K = 4
SWIGLU_ALPHA = 1.702
SWIGLU_LIMIT = 7.0
EPS = 1e-6
LAM_INIT = 0.8 - 0.6 * math.exp(-0.3 * 0)

D_NAT = 3 * CONV_WIDTH + 3 * D_MODEL
COL_K, COL_GA, COL_GB = 0, 1024, 2048
COL_CB, COL_CC, COL_CU = 3072, 3584, 4096
D_TR = 2 * D_MODEL
ROW_QT, ROW_VT = 0, 1024

TM_IN = 1024
TN_IN = 1152
TQ = 512
HEADS_PER_STEP = 2
TM_POST = 512
T_SCAN = 512
SC_WINDOW = 128
ROW_BLOCK = 512
MASK_NEG = -1e30
ONES_ROWS = 16
LANES = 128
DEINT_CHUNK = 512
VMEM_LIMIT = 56 * 1024 * 1024


def _sigmoid(x):
    return 1.0 / (1.0 + jnp.exp(-x))


def _rms(x, g):
    return x * lax.rsqrt(jnp.mean(x * x, axis=-1, keepdims=True) + EPS) * g


def _pack_halves(v):
    bits = pltpu.bitcast(v.astype(BF16).astype(F32), jnp.uint32)
    w = v.shape[1] // 2
    return (bits[:, w:] & jnp.uint32(0xFFFF0000)) | (bits[:, :w] >> 16)


def _unpack_halves(words):
    return (pltpu.bitcast(words << 16, F32), pltpu.bitcast(words & jnp.uint32(0xFFFF0000), F32))


def _ada_kernel(c_ref, w_ref, b_ref, lq1_ref, lk1_ref, lq2_ref, lk2_ref, mod_ref, lam_ref):
    c = c_ref[...]
    s = c * _sigmoid(c)
    mod_ref[...] = jnp.dot(s, w_ref[...], precision=lax.Precision.HIGHEST,
                           preferred_element_type=F32) + b_ref[...]
    a1 = jnp.sum(lq1_ref[...] * lk1_ref[...], axis=-1, keepdims=True)
    a2 = jnp.sum(lq2_ref[...] * lk2_ref[...], axis=-1, keepdims=True)
    lam_ref[...] = jnp.broadcast_to(jnp.exp(a1) - jnp.exp(a2) + LAM_INIT, lam_ref.shape)


def _ada(c, w_ada, b_ada, lq1, lk1, lq2, lk2):
    bsz = c.shape[0]
    n_chunks = w_ada.shape[1] // D_MODEL
    vec = pl.BlockSpec((1, HEAD_DIM), lambda j: (0, 0))
    return pl.pallas_call(
        _ada_kernel,
        grid=(n_chunks,),
        in_specs=[pl.BlockSpec((bsz, D_MODEL), lambda j: (0, 0)),
                  pl.BlockSpec((D_MODEL, D_MODEL), lambda j: (0, j)),
                  pl.BlockSpec((1, D_MODEL), lambda j: (0, j)),
                  vec, vec, vec, vec],
        out_specs=[pl.BlockSpec((bsz, D_MODEL), lambda j: (0, j)),
                   pl.BlockSpec((1, 128), lambda j: (0, 0))],
        out_shape=[jax.ShapeDtypeStruct((bsz, n_chunks * D_MODEL), F32),
                   jax.ShapeDtypeStruct((1, 128), F32)],
        name="ada",
    )(c, w_ada, b_ada.reshape(1, -1), lq1.reshape(1, -1), lk1.reshape(1, -1),
      lq2.reshape(1, -1), lk2.reshape(1, -1))


def _inproj_kernel(x_ref, mod_ref, g_ref, w_ref, wt_ref, o_ref, ot_ref, h_scr, ht_scr, *, n_nat):
    j = pl.program_id(2)

    @pl.when(j == 0)
    def _():
        h = _rms(x_ref[0], g_ref[...]) * (1.0 + mod_ref[0, 1:2, :]) + mod_ref[0, 0:1, :]
        h_scr[...] = h.astype(BF16)
        ht_scr[...] = h.T.astype(BF16)

    @pl.when(j < n_nat)
    def _():
        o_ref[0] = jnp.dot(h_scr[...], w_ref[...], preferred_element_type=F32).astype(BF16)

    @pl.when(j == n_nat)
    def _():
        ot_ref[0] = jnp.dot(wt_ref[...], ht_scr[...], preferred_element_type=F32).astype(BF16)


def _inproj(x, mod3, norm1_g, w_nat, w_tr):
    bsz, seq, _ = x.shape
    tm = min(TM_IN, seq)
    n_nat = D_NAT // TN_IN
    nat_j = lambda j: jnp.minimum(j, n_nat - 1)
    return pl.pallas_call(
        functools.partial(_inproj_kernel, n_nat=n_nat),
        grid=(bsz, seq // tm, n_nat + 1),
        in_specs=[pl.BlockSpec((1, tm, D_MODEL), lambda b, i, j: (b, i, 0)),
                  pl.BlockSpec((1, 6, D_MODEL), lambda b, i, j: (b, 0, 0)),
                  pl.BlockSpec((1, D_MODEL), lambda b, i, j: (0, 0)),
                  pl.BlockSpec((D_MODEL, TN_IN), lambda b, i, j: (0, nat_j(j))),
                  pl.BlockSpec((D_TR, D_MODEL), lambda b, i, j: (0, 0))],
        out_specs=[pl.BlockSpec((1, tm, TN_IN), lambda b, i, j: (b, i, nat_j(j))),
                   pl.BlockSpec((1, D_TR, tm), lambda b, i, j: (b, 0, i))],
        out_shape=[jax.ShapeDtypeStruct((bsz, seq, D_NAT), BF16),
                   jax.ShapeDtypeStruct((bsz, D_TR, seq), BF16)],
        scratch_shapes=[pltpu.VMEM((tm, D_MODEL), BF16), pltpu.VMEM((D_MODEL, tm), BF16)],
        compiler_params=pltpu.CompilerParams(
            dimension_semantics=("arbitrary", "arbitrary", "arbitrary"),
            vmem_limit_bytes=VMEM_LIMIT),
        name="inproj",
    )(x, mod3, norm1_g.reshape(1, -1), w_nat, w_tr)


def _attn_kernel(lam_ref, qt_ref, k_ref, vt_ref, bias_ref, g_ref, o_ref,
                 m_scr, acc_scr, vte_scr, s_a, s_b, t_a, t_b, *, tq):
    i = pl.program_id(2)
    heads = range(HEADS_PER_STEP)
    head_rows = lambda hh: slice(hh * V_DIM, (hh + 1) * V_DIM)

    @pl.when(i == 0)
    def _():
        for hh in heads:
            vte_scr[hh, 0:V_DIM, :] = vt_ref[0, head_rows(hh), :]
            vte_scr[hh, V_DIM:, :] = jnp.ones((ONES_ROWS, vte_scr.shape[2]), BF16)

    qt_maps = []
    for hh in heads:
        qt = qt_ref[0, head_rows(hh), :] * jnp.asarray(HEAD_DIM ** -0.5, BF16)
        row = lax.broadcasted_iota(jnp.int32, qt.shape, 0)
        zero = jnp.zeros_like(qt)
        qt_maps += [jnp.where(row < HEAD_DIM, qt, zero), jnp.where(row >= HEAD_DIM, qt, zero)]

    m_scr[...] = jnp.full(m_scr.shape, MASK_NEG, F32)
    acc_scr[...] = jnp.zeros(acc_scr.shape, F32)
    slots = ((s_a, t_a), (s_b, t_b))

    def produce(j, bias, slot):
        s_ref, t_ref = slots[slot]
        for hh in heads:
            k = k_ref[0, pl.ds(pl.multiple_of(j * tq, tq), tq), head_rows(hh)]
            for m in range(2):
                c = 2 * hh + m
                s = jnp.dot(k, qt_maps[c], preferred_element_type=F32)
                if bias is not None:
                    s = s + bias_ref[hh, bias]
                s_ref[c] = s
                t_ref[c] = jnp.max(s, axis=0, keepdims=True)

    def consume(j, slot):
        s_ref, t_ref = slots[slot]
        for hh in heads:
            vte = vte_scr[hh, :, pl.ds(pl.multiple_of(j * tq, tq), tq)]
            for m in range(2):
                c = 2 * hh + m
                m_prev = m_scr[c]
                m_new = jnp.maximum(m_prev, t_ref[c])
                alpha = jnp.exp(m_prev - m_new)
                p = jnp.exp(s_ref[c] - m_new).astype(BF16)
                acc_scr[c] = alpha * acc_scr[c] + jnp.dot(vte, p, preferred_element_type=F32)
                m_scr[c] = m_new

    diag, off1 = 0, 1

    @pl.when(i == 0)
    def _():
        produce(0, diag, 0)
        consume(0, 0)

    @pl.when(i == 1)
    def _():
        produce(0, off1, 0)
        produce(1, diag, 1)
        consume(0, 0)
        consume(1, 1)

    @pl.when(i >= 2)
    def _():
        produce(0, None, 0)

        def pair(r, carry):
            t = 2 * r
            produce(t + 1, None, 1)
            consume(t, 0)
            produce(t + 2, None, 0)
            consume(t + 1, 1)
            return carry

        lax.fori_loop(0, (i - 2) // 2, pair, 0)

        @pl.when(i % 2 == 0)
        def _():
            produce(i - 1, off1, 1)
            consume(i - 2, 0)
            produce(i, diag, 0)
            consume(i - 1, 1)
            consume(i, 0)

        @pl.when(i % 2 == 1)
        def _():
            produce(i - 2, None, 1)
            consume(i - 3, 0)
            produce(i - 1, off1, 0)
            consume(i - 2, 1)
            produce(i, diag, 1)
            consume(i - 1, 0)
            consume(i, 1)

    lam = lam_ref[0]
    for hh in heads:
        a1, a2 = acc_scr[2 * hh], acc_scr[2 * hh + 1]
        ot = a1[:V_DIM] / a1[V_DIM:V_DIM + 1] - lam * (a2[:V_DIM] / a2[V_DIM:V_DIM + 1])
        ot = ot * lax.rsqrt(jnp.mean(ot * ot, axis=0, keepdims=True) + EPS) * g_ref[...]
        o_ref[0, :, head_rows(hh)] = (ot * (1.0 - LAM_INIT)).T.astype(BF16)


def _attention(proj, proj_t, lam, bias_tiles, subln_g):
    bsz, seq, _ = proj.shape
    tq = min(TQ, seq)
    nq = seq // tq
    hps = HEADS_PER_STEP
    width = hps * V_DIM
    kern = functools.partial(_attn_kernel, tq=tq)
    score = pltpu.VMEM((2 * hps, tq, tq), F32)
    colmax = pltpu.VMEM((2 * hps, 1, tq), F32)
    return pl.pallas_call(
        kern,
        grid=(bsz, N_HEADS // hps, nq),
        in_specs=[pl.BlockSpec(memory_space=pltpu.SMEM),
                  pl.BlockSpec((1, width, tq), lambda b, h, i: (b, ROW_QT // width + h, i)),
                  pl.BlockSpec((1, seq, width), lambda b, h, i: (b, 0, COL_K // width + h)),
                  pl.BlockSpec((1, width, seq), lambda b, h, i: (b, ROW_VT // width + h, 0)),
                  pl.BlockSpec((hps, 2, tq, tq), lambda b, h, i: (h, 0, 0, 0)),
                  pl.BlockSpec((V_DIM, 1), lambda b, h, i: (0, 0))],
        out_specs=pl.BlockSpec((1, tq, width), lambda b, h, i: (b, i, h)),
        out_shape=jax.ShapeDtypeStruct((bsz, seq, N_HEADS * V_DIM), BF16),
        scratch_shapes=[pltpu.VMEM((2 * hps, 1, tq), F32),
                        pltpu.VMEM((2 * hps, V_DIM + ONES_ROWS, tq), F32),
                        pltpu.VMEM((hps, V_DIM + ONES_ROWS, seq), BF16),
                        score, score, colmax, colmax],
        compiler_params=pltpu.CompilerParams(
            dimension_semantics=("arbitrary", "arbitrary", "arbitrary"),
            vmem_limit_bytes=VMEM_LIMIT),
        name="attn",
    )(lam, proj_t, proj, proj_t, bias_tiles, subln_g.reshape(-1, 1))


def _t5_bucket(dist):
    max_exact = N_BUCKETS // 2
    d = jnp.maximum(dist, 1).astype(F32)
    large = max_exact + (jnp.log(d / max_exact) / math.log(MAX_DISTANCE / max_exact)
                         * (N_BUCKETS - max_exact)).astype(jnp.int32)
    large = jnp.minimum(large, N_BUCKETS - 1)
    return jnp.where(dist < max_exact, dist, large)


def _bias_tiles(rel_table, tq):
    assert tq >= MAX_DISTANCE
    r = jnp.arange(tq, dtype=jnp.int32)
    rel0 = r[None, :] - r[:, None]
    rel = jnp.stack([rel0, rel0 + tq])
    table = (rel_table - rel_table[N_BUCKETS - 1:N_BUCKETS, :]).astype(F32)
    bucket = _t5_bucket(jnp.maximum(rel, 0))
    onehot = (bucket[None] == jnp.arange(N_BUCKETS, dtype=jnp.int32)[:, None, None, None]).astype(F32)
    bias = jnp.einsum('kh,kabc->habc', table, onehot, precision=lax.Precision.HIGHEST)
    return jnp.where((rel >= 0)[None], bias, MASK_NEG)


def _post_kernel(att_ref, ga_ref, gb_ref, cb_ref, cc_ref, cu_ref, x_ref, mod_ref,
                 cw_ref, cbias_ref, woc_ref, woa_ref, wmg_ref, g2_ref, wr_ref, br_ref,
                 x1_ref, hp_ref, idx_ref, tw_ref, cnt_ref,
                 ubuf, carry, *, tm, tiles_per_seq):
    i = pl.program_id(0)

    @pl.when(i % tiles_per_seq == 0)
    def _():
        carry[...] = jnp.zeros(carry.shape, F32)

    @pl.when(i == 0)
    def _():
        cnt_ref[...] = jnp.zeros(cnt_ref.shape, F32)

    u = cc_ref[...].astype(F32) * cu_ref[...].astype(F32)
    ubuf[0:8, :] = carry[...]
    ubuf[8:8 + tm, :] = u
    carry[...] = u[tm - 8:tm, :]
    conv = (cbias_ref[...] + ubuf[6:6 + tm, :] * cw_ref[0:1, :]
            + ubuf[7:7 + tm, :] * cw_ref[1:2, :] + u * cw_ref[2:3, :])
    ya = jnp.dot((cb_ref[...].astype(F32) * conv).astype(BF16), woc_ref[...],
                 preferred_element_type=F32)
    yb = jnp.dot(att_ref[...], woa_ref[...], preferred_element_type=F32)
    merged = _sigmoid(ga_ref[...].astype(F32)) * ya + _sigmoid(gb_ref[...].astype(F32)) * yb
    mix = jnp.dot(merged.astype(BF16), wmg_ref[...], preferred_element_type=F32)
    x1 = x_ref[...] + mod_ref[0, 2:3, :] * mix
    x1_ref[...] = x1

    h2 = _rms(x1, g2_ref[...]) * (1.0 + mod_ref[0, 4:5, :]) + mod_ref[0, 3:4, :]
    h_hi = h2.astype(BF16)
    h_hi32 = h_hi.astype(F32)
    h_lo = (h2 - h_hi32).astype(BF16)

    part = jnp.dot(h_hi, wr_ref[...], preferred_element_type=F32)
    logits = (part[:, :N_EXPERTS] + part[:, N_EXPERTS:]
              + jnp.dot(h_lo, wr_ref[:, :N_EXPERTS], preferred_element_type=F32)
              + br_ref[...])

    lane = lax.broadcasted_iota(jnp.int32, logits.shape, 1)
    work = logits
    sel_l, sel_i = [], []
    member = jnp.zeros(logits.shape, F32)
    for _ in range(TOP_K):
        mx = jnp.max(work, axis=-1, keepdims=True)
        ix = jnp.min(jnp.where(work == mx, lane, N_EXPERTS), axis=-1, keepdims=True)
        hit = lane == ix
        member = member + hit.astype(F32)
        work = jnp.where(hit, -jnp.inf, work)
        sel_l.append(mx)
        sel_i.append(ix)
    ex = [jnp.exp(l - sel_l[0]) for l in sel_l]
    den = ex[0] + ex[1] + ex[2] + ex[3]
    tw_ref[...] = jnp.concatenate([e / den for e in ex], axis=1)
    idx_ref[...] = jnp.concatenate(sel_i, axis=1)
    cnt_ref[...] += jnp.sum(member, axis=0, keepdims=True)

    hp_ref[...] = _pack_halves(h2)


def _post(att, proj, x2, mod3, conv_w, conv_b, woc_b, woa_b, wmg_b, norm2_g, wr_split, b_router, seq):
    n = x2.shape[0]
    tm = min(TM_POST, seq)
    tiles_per_seq = seq // tm
    kern = functools.partial(_post_kernel, tm=tm, tiles_per_seq=tiles_per_seq)
    row = lambda c: (lambda i: (i, c))
    const = lambda i: (0, 0)
    return pl.pallas_call(
        kern,
        grid=(n // tm,),
        in_specs=[pl.BlockSpec((tm, D_MODEL), row(0)),
                  pl.BlockSpec((tm, D_MODEL), row(COL_GA // D_MODEL)),
                  pl.BlockSpec((tm, D_MODEL), row(COL_GB // D_MODEL)),
                  pl.BlockSpec((tm, CONV_WIDTH), row(COL_CB // CONV_WIDTH)),
                  pl.BlockSpec((tm, CONV_WIDTH), row(COL_CC // CONV_WIDTH)),
                  pl.BlockSpec((tm, CONV_WIDTH), row(COL_CU // CONV_WIDTH)),
                  pl.BlockSpec((tm, D_MODEL), row(0)),
                  pl.BlockSpec((1, 6, D_MODEL), lambda i: (i // tiles_per_seq, 0, 0)),
                  pl.BlockSpec((CONV_K, CONV_WIDTH), const),
                  pl.BlockSpec((1, CONV_WIDTH), const),
                  pl.BlockSpec((CONV_WIDTH, D_MODEL), const),
                  pl.BlockSpec((D_MODEL, D_MODEL), const),
                  pl.BlockSpec((D_MODEL, D_MODEL), const),
                  pl.BlockSpec((1, D_MODEL), const),
                  pl.BlockSpec((D_MODEL, 2 * N_EXPERTS), const),
                  pl.BlockSpec((1, N_EXPERTS), const)],
        out_specs=[pl.BlockSpec((tm, D_MODEL), row(0)),
                   pl.BlockSpec((tm, D_MODEL // 2), row(0)),
                   pl.BlockSpec((tm, TOP_K), row(0)),
                   pl.BlockSpec((tm, TOP_K), row(0)),
                   pl.BlockSpec((1, N_EXPERTS), const)],
        out_shape=[jax.ShapeDtypeStruct((n, D_MODEL), F32),
                   jax.ShapeDtypeStruct((n, D_MODEL // 2), jnp.uint32),
                   jax.ShapeDtypeStruct((n, TOP_K), jnp.int32),
                   jax.ShapeDtypeStruct((n, TOP_K), F32),
                   jax.ShapeDtypeStruct((1, N_EXPERTS), F32)],
        scratch_shapes=[pltpu.VMEM((tm + 8, CONV_WIDTH), F32), pltpu.VMEM((8, CONV_WIDTH), F32)],
        compiler_params=pltpu.CompilerParams(dimension_semantics=("arbitrary",),
                                             vmem_limit_bytes=VMEM_LIMIT),
        name="post",
    )(att, proj, proj, proj, proj, proj, x2, mod3, conv_w, conv_b.reshape(1, -1),
      woc_b, woa_b, wmg_b, norm2_g.reshape(1, -1), wr_split, b_router.reshape(1, -1))


def _scan_kernel(idx_ref, pstart_ref, dest_ref, carry, *, t):
    @pl.when(pl.program_id(0) == 0)
    def _():
        carry[...] = jnp.zeros(carry.shape, F32)

    idx = idx_ref[...]
    lane = lax.broadcasted_iota(jnp.int32, (t, N_EXPERTS), 1)
    hits = [lane == idx[:, k:k + 1] for k in range(TOP_K)]
    member = hits[0].astype(F32)
    for k in range(1, TOP_K):
        member = member + hits[k].astype(F32)
    r = lax.broadcasted_iota(jnp.int32, (t, t), 0)
    c = lax.broadcasted_iota(jnp.int32, (t, t), 1)
    earlier = jnp.where(c < r, 1.0, 0.0).astype(BF16)
    rank = jnp.dot(earlier, member.astype(BF16), preferred_element_type=F32) + carry[...]
    dest_e = pstart_ref[...] + rank
    cols = [jnp.sum(jnp.where(hits[k], dest_e, 0.0), axis=-1, keepdims=True) for k in range(TOP_K)]
    dest_ref[...] = jnp.concatenate(cols, axis=1).astype(jnp.int32)
    carry[...] += jnp.sum(member, axis=0, keepdims=True)


def _scan(idx, pstart):
    n = idx.shape[0]
    t = min(T_SCAN, n)
    return pl.pallas_call(
        functools.partial(_scan_kernel, t=t),
        grid=(n // t,),
        in_specs=[pl.BlockSpec((t, TOP_K), lambda i: (i, 0)),
                  pl.BlockSpec((1, N_EXPERTS), lambda i: (0, 0))],
        out_specs=pl.BlockSpec((t, TOP_K), lambda i: (i, 0)),
        out_shape=jax.ShapeDtypeStruct((n, TOP_K), jnp.int32),
        scratch_shapes=[pltpu.VMEM((1, N_EXPERTS), F32)],
        compiler_params=pltpu.CompilerParams(dimension_semantics=("arbitrary",)),
        name="scan",
    )(idx, pstart)


def _sc_mesh():
    return plsc.VectorSubcoreMesh(core_axis_name="core", subcore_axis_name="subcore")


def _sc_worker(mesh):
    return lax.axis_index("core") * mesh.num_subcores + lax.axis_index("subcore")


def _sc_dispatch(dest_t, hp, rows):
    n, width = hp.shape
    mesh = _sc_mesh()
    workers = mesh.num_cores * mesh.num_subcores
    per_worker = n // (SC_WINDOW * workers)
    assert per_worker * SC_WINDOW * workers == n

    @pl.kernel(out_type=jax.ShapeDtypeStruct((rows, width), hp.dtype), mesh=mesh,
               scratch_types=[pltpu.VMEM((SC_WINDOW, width), hp.dtype),
                              pltpu.VMEM((TOP_K, SC_WINDOW), jnp.int32)])
    def scatter(hp_hbm, dest_hbm, xs_hbm, row_buf, idx_buf):
        first = _sc_worker(mesh) * per_worker

        @pl.loop(0, per_worker)
        def _(w):
            t0 = (first + w) * SC_WINDOW
            pltpu.sync_copy(hp_hbm.at[pl.ds(t0, SC_WINDOW)], row_buf)
            pltpu.sync_copy(dest_hbm.at[:, pl.ds(t0, SC_WINDOW)], idx_buf)
            for k in range(TOP_K):
                pltpu.sync_copy(row_buf, xs_hbm.at[idx_buf.at[k]])

    return scatter(hp, dest_t)


def _sc_gather(src_rows, table):
    n_out = src_rows.shape[1]
    width = table.shape[1]
    mesh = _sc_mesh()
    workers = mesh.num_cores * mesh.num_subcores
    per_worker = n_out // (SC_WINDOW * workers)
    assert per_worker * SC_WINDOW * workers == n_out

    @pl.kernel(out_type=jax.ShapeDtypeStruct((n_out, width), table.dtype), mesh=mesh,
               scratch_types=[pltpu.VMEM((SC_WINDOW, width), table.dtype),
                              pltpu.VMEM((1, SC_WINDOW), jnp.int32)])
    def gather(table_hbm, src_hbm, out_hbm, row_buf, idx_buf):
        first = _sc_worker(mesh) * per_worker

        @pl.loop(0, per_worker)
        def _(w):
            r0 = (first + w) * SC_WINDOW
            pltpu.sync_copy(src_hbm.at[:, pl.ds(r0, SC_WINDOW)], idx_buf)
            pltpu.sync_copy(table_hbm.at[idx_buf.at[0]], row_buf)
            pltpu.sync_copy(row_buf, out_hbm.at[pl.ds(r0, SC_WINDOW)])

    return gather(table, src_rows)


def _expert_kernel(be_ref, first_ref, used_ref, valid_ref, xs_ref, wgu_ref, bg_ref, bl_ref, wd_ref,
                   bd_ref, y_ref, wg_s, wl_s, wd_s, tmp_s):
    del be_ref
    i = pl.program_id(0)

    @pl.when(first_ref[i] == 1)
    def _():
        half = DEINT_CHUNK // 2
        for c in range(wgu_ref.shape[2] // DEINT_CHUNK):
            for db in range(D_MODEL // LANES):
                rows = slice(db * LANES, (db + 1) * LANES)
                tmp_s[...] = wgu_ref[0, rows, c * DEINT_CHUNK:(c + 1) * DEINT_CHUNK].T
                cols = slice(c * half, (c + 1) * half)
                wg_s[rows, cols] = tmp_s[pl.ds(0, half, stride=2), :].T.astype(BF16)
                wl_s[rows, cols] = tmp_s[pl.ds(1, half, stride=2), :].T.astype(BF16)
        wd_s[...] = wd_ref[0].astype(BF16)

    @pl.when(i < used_ref[0])
    def _():
        xp = xs_ref[...]
        rid = lax.broadcasted_iota(jnp.int32, xp.shape, 0)
        xp = jnp.where(rid < valid_ref[i], xp, jnp.uint32(0))
        xb = jnp.concatenate(_unpack_halves(xp), axis=1).astype(BF16)
        g = jnp.dot(xb, wg_s[...], preferred_element_type=F32) + bg_ref[0]
        l = jnp.dot(xb, wl_s[...], preferred_element_type=F32) + bl_ref[0]
        glu = jnp.minimum(g, SWIGLU_LIMIT)
        lin = jnp.clip(l, -SWIGLU_LIMIT, SWIGLU_LIMIT)
        act = glu * _sigmoid(SWIGLU_ALPHA * glu) * (lin + 1.0)
        y = jnp.dot(act.astype(BF16), wd_s[...], preferred_element_type=F32) + bd_ref[0]
        y_ref[...] = _pack_halves(y)

    @pl.when(i >= used_ref[0])
    def _():
        y_ref[...] = jnp.zeros(y_ref.shape, y_ref.dtype)


def _experts(block_e, first, n_used, valid, xs, wgu, bg, bl, wd, bd):
    rows, width = xs.shape
    d_exp = wd.shape[1]
    n_blocks = rows // ROW_BLOCK
    wsel = lambda i, be, fi, nu, va: (be[i], 0, 0)
    rsel = lambda i, be, fi, nu, va: (jnp.minimum(i, nu[0] - 1), 0)
    grid_spec = pltpu.PrefetchScalarGridSpec(
        num_scalar_prefetch=4,
        grid=(n_blocks,),
        in_specs=[pl.BlockSpec((ROW_BLOCK, width), rsel),
                  pl.BlockSpec((1, D_MODEL, 2 * d_exp), wsel),
                  pl.BlockSpec((1, 1, d_exp), wsel),
                  pl.BlockSpec((1, 1, d_exp), wsel),
                  pl.BlockSpec((1, d_exp, D_MODEL), wsel),
                  pl.BlockSpec((1, 1, D_MODEL), wsel)],
        out_specs=pl.BlockSpec((ROW_BLOCK, width), lambda i, be, fi, nu, va: (i, 0)),
        scratch_shapes=[pltpu.VMEM((D_MODEL, d_exp), BF16), pltpu.VMEM((D_MODEL, d_exp), BF16),
                        pltpu.VMEM((d_exp, D_MODEL), BF16), pltpu.VMEM((DEINT_CHUNK, LANES), F32)],
    )
    return pl.pallas_call(
        _expert_kernel,
        grid_spec=grid_spec,
        out_shape=jax.ShapeDtypeStruct((rows, width), xs.dtype),
        compiler_params=pltpu.CompilerParams(dimension_semantics=("arbitrary",),
                                             vmem_limit_bytes=VMEM_LIMIT),
        name="experts",
    )(block_e, first, n_used, valid, xs, wgu, bg, bl, wd, bd)


def _combine_kernel(y0_ref, y1_ref, y2_ref, y3_ref, x1_ref, tw_ref, mod_ref, gf_ref, o_ref):
    tw = tw_ref[...]
    lo = hi = None
    for k, yk_ref in enumerate((y0_ref, y1_ref, y2_ref, y3_ref)):
        y_lo, y_hi = _unpack_halves(yk_ref[...])
        w = tw[:, k:k + 1]
        lo = w * y_lo if lo is None else lo + w * y_lo
        hi = w * y_hi if hi is None else hi + w * y_hi
    moe = jnp.concatenate([lo, hi], axis=1)
    xo = x1_ref[...] + mod_ref[0, 5:6, :] * moe
    o_ref[...] = _rms(xo, gf_ref[...])


def _combine(yk, x1, tw, mod3, normf_g, seq):
    n = x1.shape[0]
    t = min(TM_POST, seq)
    tiles_per_seq = seq // t
    slot = lambda k: pl.BlockSpec((t, yk.shape[1]), lambda i: (k * (n // t) + i, 0))
    return pl.pallas_call(
        _combine_kernel,
        grid=(n // t,),
        in_specs=[slot(0), slot(1), slot(2), slot(3),
                  pl.BlockSpec((t, D_MODEL), lambda i: (i, 0)),
                  pl.BlockSpec((t, TOP_K), lambda i: (i, 0)),
                  pl.BlockSpec((1, 6, D_MODEL), lambda i: (i // tiles_per_seq, 0, 0)),
                  pl.BlockSpec((1, D_MODEL), lambda i: (0, 0))],
        out_specs=pl.BlockSpec((t, D_MODEL), lambda i: (i, 0)),
        out_shape=jax.ShapeDtypeStruct((n, D_MODEL), F32),
        compiler_params=pltpu.CompilerParams(dimension_semantics=("arbitrary",)),
        name="combine",
    )(yk, yk, yk, yk, x1, tw, mod3, normf_g.reshape(1, -1))


def _in_proj_weights(w_in):
    c0 = 3 * CONV_WIDTH
    q, k, v = (w_in[:, c0 + s * D_MODEL:c0 + (s + 1) * D_MODEL] for s in range(3))
    w_nat = jnp.concatenate([k, w_in[:, c0 + 3 * D_MODEL:], w_in[:, :c0]], axis=1).astype(BF16)
    w_tr = jnp.concatenate([q, v], axis=1).T.astype(BF16)
    return w_nat, w_tr


def kernel(x, c, w_ada, b_ada, norm1_g, norm2_g, w_in, conv_w, conv_b, w_out_conv, lambda_q1, lambda_k1, lambda_q2, lambda_k2, subln_g, w_o_attn, w_merge, w_router, b_router, w_gate_up, b_gate_up, w_down, b_down, rel_bias_table, normf_g):
    bsz, seq, _ = x.shape
    n = bsz * seq
    assert w_ada.shape[0] == 1, "single layer"
    assert seq % min(TQ, seq) == 0 and n % TM_POST == 0 and (n * TOP_K) % ROW_BLOCK == 0

    mod, lam = _ada(c, w_ada[0], b_ada[0], lambda_q1[0], lambda_k1[0], lambda_q2[0], lambda_k2[0])
    mod3 = mod.reshape(bsz, 6, D_MODEL)

    proj, proj_t = _inproj(x, mod3, norm1_g[0], *_in_proj_weights(w_in[0]))
    att = _attention(proj, proj_t, lam[0, :1], _bias_tiles(rel_bias_table, min(TQ, seq)), subln_g[0])

    wr = w_router[0]
    wr_hi = wr.astype(BF16)
    wr_lo = (wr - wr_hi.astype(F32)).astype(BF16)
    x1, hp, idx, tw, counts = _post(
        att.reshape(n, -1), proj.reshape(n, -1), x.reshape(n, -1), mod3, conv_w[0], conv_b[0],
        w_out_conv[0].astype(BF16), w_o_attn[0].astype(BF16), w_merge[0].astype(BF16),
        norm2_g[0], jnp.concatenate([wr_hi, wr_lo], axis=1), b_router[0], seq)

    cnt = counts[0].astype(jnp.int32)
    padded = (cnt + ROW_BLOCK - 1) // ROW_BLOCK * ROW_BLOCK
    pend = jnp.cumsum(padded)
    pstart = pend - padded
    n_blocks = n * TOP_K // ROW_BLOCK + N_EXPERTS
    rows = n_blocks * ROW_BLOCK
    block_start = jnp.arange(n_blocks, dtype=jnp.int32) * ROW_BLOCK
    block_e = jnp.minimum(jnp.sum(pend[None, :] <= block_start[:, None], axis=1),
                          N_EXPERTS - 1).astype(jnp.int32)

    dest_t = _scan(idx, pstart.astype(F32).reshape(1, -1)).T
    xs = _sc_dispatch(dest_t, hp, rows)

    n_used = (pend[-1:] // ROW_BLOCK).astype(jnp.int32)
    changed = jnp.concatenate([jnp.ones((1,), jnp.bool_), block_e[1:] != block_e[:-1]])
    first = (changed & (jnp.arange(n_blocks) < n_used[0])).astype(jnp.int32)
    valid = jnp.clip((pstart + cnt)[block_e] - block_start, 0, ROW_BLOCK).astype(jnp.int32)
    bgu = b_gate_up[0]
    y = _experts(block_e, first, n_used, valid, xs, w_gate_up[0], bgu[:, None, 0::2],
                 bgu[:, None, 1::2], w_down[0], b_down[0][:, None, :])

    yk = _sc_gather(dest_t.reshape(1, -1), y)
    out = _combine(yk, x1, tw, mod3, normf_g, seq)
    return out.reshape(bsz, seq, D_MODEL)
```

```python
import functools
import math

import jax
import jax.numpy as jnp
from jax import lax
from jax.experimental import pallas as pl
from jax.experimental.pallas import tpu as pltpu
from jax.experimental.pallas import tpu_sc as plsc

F32 = jnp.float32
BF16 = jnp.bfloat16

D_MODEL = 1024
CONV_WIDTH = 512
CONV_K = 3
N_HEADS = 8
HEAD_DIM = 64
V_DIM = 2 * HEAD_DIM
N_BUCKETS = 32
MAX_DISTANCE = 128
N_EXPERTS = 32
TOP_K = 4
SWIGLU_ALPHA = 1.702
SWIGLU_LIMIT = 7.0
EPS = 1e-6
LAM_INIT = 0.8 - 0.6 * math.exp(-0.3 * 0)

D_NAT = 3 * CONV_WIDTH + 3 * D_MODEL
COL_K, COL_GA, COL_GB = 0, 1024, 2048
COL_CB, COL_CC, COL_CU = 3072, 3584, 4096
D_TR = 2 * D_MODEL
ROW_QT, ROW_VT = 0, 1024

TM_IN = 1024
TN_IN = 1536
TQ = 512
HEADS_PER_STEP = 2
TM_POST = 512
T_SCAN = 512
SC_WINDOW = 128
COMBINE_CHUNKS = 2
ROW_BLOCK = 512
MASK_NEG = -1e30
ONES_ROWS = 16
LANES = 128
DEINT_CHUNK = 512
VMEM_LIMIT = 56 * 1024 * 1024


def _sigmoid(x):
    return 1.0 / (1.0 + jnp.exp(-x))


def _rms(x, g):
    return x * lax.rsqrt(jnp.mean(x * x, axis=-1, keepdims=True) + EPS) * g


def _pack_halves(v):
    bits = pltpu.bitcast(v.astype(BF16).astype(F32), jnp.uint32)
    w = v.shape[1] // 2
    return (bits[:, w:] & jnp.uint32(0xFFFF0000)) | (bits[:, :w] >> 16)


def _unpack_halves(words):
    return (pltpu.bitcast(words << 16, F32), pltpu.bitcast(words & jnp.uint32(0xFFFF0000), F32))


def _ada_kernel(c_ref, w_ref, b_ref, lq1_ref, lk1_ref, lq2_ref, lk2_ref, mod_ref, lam_ref):
    c = c_ref[...]
    s = c * _sigmoid(c)
    mod_ref[...] = jnp.dot(s, w_ref[...], precision=lax.Precision.HIGHEST,
                           preferred_element_type=F32) + b_ref[...]
    a1 = jnp.sum(lq1_ref[...] * lk1_ref[...], axis=-1, keepdims=True)
    a2 = jnp.sum(lq2_ref[...] * lk2_ref[...], axis=-1, keepdims=True)
    lam_ref[...] = jnp.broadcast_to(jnp.exp(a1) - jnp.exp(a2) + LAM_INIT, lam_ref.shape)


def _ada(c, w_ada, b_ada, lq1, lk1, lq2, lk2):
    bsz = c.shape[0]
    n_chunks = w_ada.shape[1] // D_MODEL
    vec = pl.BlockSpec((1, HEAD_DIM), lambda j: (0, 0))
    return pl.pallas_call(
        _ada_kernel,
        grid=(n_chunks,),
        in_specs=[pl.BlockSpec((bsz, D_MODEL), lambda j: (0, 0)),
                  pl.BlockSpec((D_MODEL, D_MODEL), lambda j: (0, j)),
                  pl.BlockSpec((1, D_MODEL), lambda j: (0, j)),
                  vec, vec, vec, vec],
        out_specs=[pl.BlockSpec((bsz, D_MODEL), lambda j: (0, j)),
                   pl.BlockSpec((1, 128), lambda j: (0, 0))],
        out_shape=[jax.ShapeDtypeStruct((bsz, n_chunks * D_MODEL), F32),
                   jax.ShapeDtypeStruct((1, 128), F32)],
        name="ada",
    )(c, w_ada, b_ada.reshape(1, -1), lq1.reshape(1, -1), lk1.reshape(1, -1),
      lq2.reshape(1, -1), lk2.reshape(1, -1))


def _inproj_kernel(x_ref, mod_ref, g_ref, w_ref, wt_ref, o_ref, ot_ref, h_scr, ht_scr, *, n_nat):
    j = pl.program_id(2)

    @pl.when(j == 0)
    def _():
        h = _rms(x_ref[0], g_ref[...]) * (1.0 + mod_ref[0, 1:2, :]) + mod_ref[0, 0:1, :]
        h_scr[...] = h.astype(BF16)
        ht_scr[...] = h.T.astype(BF16)

    @pl.when(j < n_nat)
    def _():
        o_ref[0] = jnp.dot(h_scr[...], w_ref[...], preferred_element_type=F32).astype(BF16)

    @pl.when(j == n_nat)
    def _():
        ot_ref[0] = jnp.dot(wt_ref[...], ht_scr[...], preferred_element_type=F32).astype(BF16)


def _inproj(x, mod3, norm1_g, w_nat, w_tr):
    bsz, seq, _ = x.shape
    tm = min(TM_IN, seq)
    n_nat = D_NAT // TN_IN
    nat_j = lambda j: jnp.minimum(j, n_nat - 1)
    return pl.pallas_call(
        functools.partial(_inproj_kernel, n_nat=n_nat),
        grid=(bsz, seq // tm, n_nat + 1),
        in_specs=[pl.BlockSpec((1, tm, D_MODEL), lambda b, i, j: (b, i, 0)),
                  pl.BlockSpec((1, 6, D_MODEL), lambda b, i, j: (b, 0, 0)),
                  pl.BlockSpec((1, D_MODEL), lambda b, i, j: (0, 0)),
                  pl.BlockSpec((D_MODEL, TN_IN), lambda b, i, j: (0, nat_j(j))),
                  pl.BlockSpec((D_TR, D_MODEL), lambda b, i, j: (0, 0))],
        out_specs=[pl.BlockSpec((1, tm, TN_IN), lambda b, i, j: (b, i, nat_j(j))),
                   pl.BlockSpec((1, D_TR, tm), lambda b, i, j: (b, 0, i))],
        out_shape=[jax.ShapeDtypeStruct((bsz, seq, D_NAT), BF16),
                   jax.ShapeDtypeStruct((bsz, D_TR, seq), BF16)],
        scratch_shapes=[pltpu.VMEM((tm, D_MODEL), BF16), pltpu.VMEM((D_MODEL, tm), BF16)],
        compiler_params=pltpu.CompilerParams(
            dimension_semantics=("arbitrary", "arbitrary", "arbitrary"),
            vmem_limit_bytes=VMEM_LIMIT),
        name="inproj",
    )(x, mod3, norm1_g.reshape(1, -1), w_nat, w_tr)


def _attn_kernel(lam_ref, qt_ref, k_ref, vt_ref, bias_ref, g_ref, o_ref,
                 m_scr, acc_scr, vte_scr, s_a, s_b, t_a, t_b, *, tq):
    i = pl.program_id(2)
    heads = range(HEADS_PER_STEP)
    head_rows = lambda hh: slice(hh * V_DIM, (hh + 1) * V_DIM)

    @pl.when(i == 0)
    def _():
        for hh in heads:
            vte_scr[hh, 0:V_DIM, :] = vt_ref[0, head_rows(hh), :]
            vte_scr[hh, V_DIM:, :] = jnp.ones((ONES_ROWS, vte_scr.shape[2]), BF16)

    qt_maps = []
    for hh in heads:
        qt = qt_ref[0, head_rows(hh), :] * jnp.asarray(HEAD_DIM ** -0.5, BF16)
        row = lax.broadcasted_iota(jnp.int32, qt.shape, 0)
        zero = jnp.zeros_like(qt)
        qt_maps += [jnp.where(row < HEAD_DIM, qt, zero), jnp.where(row >= HEAD_DIM, qt, zero)]

    m_scr[...] = jnp.full(m_scr.shape, MASK_NEG, F32)
    acc_scr[...] = jnp.zeros(acc_scr.shape, F32)
    slots = ((s_a, t_a), (s_b, t_b))

    def produce(j, bias, slot):
        s_ref, t_ref = slots[slot]
        for hh in heads:
            k = k_ref[0, pl.ds(pl.multiple_of(j * tq, tq), tq), head_rows(hh)]
            for m in range(2):
                c = 2 * hh + m
                s = jnp.dot(k, qt_maps[c], preferred_element_type=F32)
                if bias is not None:
                    s = s + bias_ref[hh, bias]
                s_ref[c] = s
                t_ref[c] = jnp.max(s, axis=0, keepdims=True)

    def consume(j, slot):
        s_ref, t_ref = slots[slot]
        for hh in heads:
            vte = vte_scr[hh, :, pl.ds(pl.multiple_of(j * tq, tq), tq)]
            for m in range(2):
                c = 2 * hh + m
                m_prev = m_scr[c]
                m_new = jnp.maximum(m_prev, t_ref[c])
                alpha = jnp.exp(m_prev - m_new)
                p = jnp.exp(s_ref[c] - m_new).astype(BF16)
                acc_scr[c] = alpha * acc_scr[c] + jnp.dot(vte, p, preferred_element_type=F32)
                m_scr[c] = m_new

    diag, off1 = 0, 1

    @pl.when(i == 0)
    def _():
        produce(0, diag, 0)
        consume(0, 0)

    @pl.when(i == 1)
    def _():
        produce(0, off1, 0)
        produce(1, diag, 1)
        consume(0, 0)
        consume(1, 1)

    @pl.when(i >= 2)
    def _():
        produce(0, None, 0)

        def pair(r, carry):
            t = 2 * r
            produce(t + 1, None, 1)
            consume(t, 0)
            produce(t + 2, None, 0)
            consume(t + 1, 1)
            return carry

        lax.fori_loop(0, (i - 2) // 2, pair, 0)

        @pl.when(i % 2 == 0)
        def _():
            produce(i - 1, off1, 1)
            consume(i - 2, 0)
            produce(i, diag, 0)
            consume(i - 1, 1)
            consume(i, 0)

        @pl.when(i % 2 == 1)
        def _():
            produce(i - 2, None, 1)
            consume(i - 3, 0)
            produce(i - 1, off1, 0)
            consume(i - 2, 1)
            produce(i, diag, 1)
            consume(i - 1, 0)
            consume(i, 1)

    lam = lam_ref[0]
    for hh in heads:
        a1, a2 = acc_scr[2 * hh], acc_scr[2 * hh + 1]
        ot = a1[:V_DIM] / a1[V_DIM:V_DIM + 1] - lam * (a2[:V_DIM] / a2[V_DIM:V_DIM + 1])
        ot = ot * lax.rsqrt(jnp.mean(ot * ot, axis=0, keepdims=True) + EPS) * g_ref[...]
        o_ref[0, :, head_rows(hh)] = (ot * (1.0 - LAM_INIT)).T.astype(BF16)


def _attention(proj, proj_t, lam, bias_tiles, subln_g):
    bsz, seq, _ = proj.shape
    tq = min(TQ, seq)
    nq = seq // tq
    hps = HEADS_PER_STEP
    width = hps * V_DIM
    kern = functools.partial(_attn_kernel, tq=tq)
    score = pltpu.VMEM((2 * hps, tq, tq), F32)
    colmax = pltpu.VMEM((2 * hps, 1, tq), F32)
    return pl.pallas_call(
        kern,
        grid=(bsz, N_HEADS // hps, nq),
        in_specs=[pl.BlockSpec(memory_space=pltpu.SMEM),
                  pl.BlockSpec((1, width, tq), lambda b, h, i: (b, ROW_QT // width + h, i)),
                  pl.BlockSpec((1, seq, width), lambda b, h, i: (b, 0, COL_K // width + h)),
                  pl.BlockSpec((1, width, seq), lambda b, h, i: (b, ROW_VT // width + h, 0)),
                  pl.BlockSpec((hps, 2, tq, tq), lambda b, h, i: (h, 0, 0, 0)),
                  pl.BlockSpec((V_DIM, 1), lambda b, h, i: (0, 0))],
        out_specs=pl.BlockSpec((1, tq, width), lambda b, h, i: (b, i, h)),
        out_shape=jax.ShapeDtypeStruct((bsz, seq, N_HEADS * V_DIM), BF16),
        scratch_shapes=[pltpu.VMEM((2 * hps, 1, tq), F32),
                        pltpu.VMEM((2 * hps, V_DIM + ONES_ROWS, tq), F32),
                        pltpu.VMEM((hps, V_DIM + ONES_ROWS, seq), BF16),
                        score, score, colmax, colmax],
        compiler_params=pltpu.CompilerParams(
            dimension_semantics=("arbitrary", "arbitrary", "arbitrary"),
            vmem_limit_bytes=VMEM_LIMIT),
        name="attn",
    )(lam, proj_t, proj, proj_t, bias_tiles, subln_g.reshape(-1, 1))


def _t5_bucket(dist):
    max_exact = N_BUCKETS // 2
    d = jnp.maximum(dist, 1).astype(F32)
    large = max_exact + (jnp.log(d / max_exact) / math.log(MAX_DISTANCE / max_exact)
                         * (N_BUCKETS - max_exact)).astype(jnp.int32)
    large = jnp.minimum(large, N_BUCKETS - 1)
    return jnp.where(dist < max_exact, dist, large)


def _bias_tiles(rel_table, tq):
    assert tq >= MAX_DISTANCE
    r = jnp.arange(tq, dtype=jnp.int32)
    rel0 = r[None, :] - r[:, None]
    rel = jnp.stack([rel0, rel0 + tq])
    table = (rel_table - rel_table[N_BUCKETS - 1:N_BUCKETS, :]).astype(F32)
    bucket = _t5_bucket(jnp.maximum(rel, 0))
    onehot = (bucket[None] == jnp.arange(N_BUCKETS, dtype=jnp.int32)[:, None, None, None]).astype(F32)
    bias = jnp.einsum('kh,kabc->habc', table, onehot, precision=lax.Precision.HIGHEST)
    return jnp.where((rel >= 0)[None], bias, MASK_NEG)


def _post_kernel(att_ref, ga_ref, gb_ref, cb_ref, cc_ref, cu_ref, x_ref, mod_ref,
                 cw_ref, cbias_ref, woc_ref, woa_ref, wmg_ref, g2_ref, wr_ref, br_ref,
                 x1_ref, hp_ref, idx_ref, tw_ref, cnt_ref,
                 ubuf, carry, *, tm, tiles_per_seq):
    i = pl.program_id(0)

    @pl.when(i % tiles_per_seq == 0)
    def _():
        carry[...] = jnp.zeros(carry.shape, F32)

    @pl.when(i == 0)
    def _():
        cnt_ref[...] = jnp.zeros(cnt_ref.shape, F32)

    u = cc_ref[...].astype(F32) * cu_ref[...].astype(F32)
    ubuf[0:8, :] = carry[...]
    ubuf[8:8 + tm, :] = u
    carry[...] = u[tm - 8:tm, :]
    conv = (cbias_ref[...] + ubuf[6:6 + tm, :] * cw_ref[0:1, :]
            + ubuf[7:7 + tm, :] * cw_ref[1:2, :] + u * cw_ref[2:3, :])
    ya = jnp.dot((cb_ref[...].astype(F32) * conv).astype(BF16), woc_ref[...],
                 preferred_element_type=F32)
    yb = jnp.dot(att_ref[...], woa_ref[...], preferred_element_type=F32)
    merged = _sigmoid(ga_ref[...].astype(F32)) * ya + _sigmoid(gb_ref[...].astype(F32)) * yb
    mix = jnp.dot(merged.astype(BF16), wmg_ref[...], preferred_element_type=F32)
    x1 = x_ref[...] + mod_ref[0, 2:3, :] * mix
    x1_ref[...] = x1

    h2 = _rms(x1, g2_ref[...]) * (1.0 + mod_ref[0, 4:5, :]) + mod_ref[0, 3:4, :]
    h_hi = h2.astype(BF16)
    h_hi32 = h_hi.astype(F32)
    h_lo = (h2 - h_hi32).astype(BF16)

    part = jnp.dot(h_hi, wr_ref[...], preferred_element_type=F32)
    logits = (part[:, :N_EXPERTS] + part[:, N_EXPERTS:]
              + jnp.dot(h_lo, wr_ref[:, :N_EXPERTS], preferred_element_type=F32)
              + br_ref[...])

    lane = lax.broadcasted_iota(jnp.int32, logits.shape, 1)
    work = logits
    sel_l, sel_i = [], []
    member = jnp.zeros(logits.shape, F32)
    for _ in range(TOP_K):
        mx = jnp.max(work, axis=-1, keepdims=True)
        ix = jnp.min(jnp.where(work == mx, lane, N_EXPERTS), axis=-1, keepdims=True)
        hit = lane == ix
        member = member + hit.astype(F32)
        work = jnp.where(hit, -jnp.inf, work)
        sel_l.append(mx)
        sel_i.append(ix)
    ex = [jnp.exp(l - sel_l[0]) for l in sel_l]
    den = ex[0] + ex[1] + ex[2] + ex[3]
    tw_ref[...] = jnp.concatenate([e / den for e in ex], axis=1)
    idx_ref[...] = jnp.concatenate(sel_i, axis=1)
    cnt_ref[...] += jnp.sum(member, axis=0, keepdims=True)

    hp_ref[...] = _pack_halves(h2)


def _post(att, proj, x2, mod3, conv_w, conv_b, woc_b, woa_b, wmg_b, norm2_g, wr_split, b_router, seq):
    n = x2.shape[0]
    tm = min(TM_POST, seq)
    tiles_per_seq = seq // tm
    kern = functools.partial(_post_kernel, tm=tm, tiles_per_seq=tiles_per_seq)
    row = lambda c: (lambda i: (i, c))
    const = lambda i: (0, 0)
    return pl.pallas_call(
        kern,
        grid=(n // tm,),
        in_specs=[pl.BlockSpec((tm, D_MODEL), row(0)),
                  pl.BlockSpec((tm, D_MODEL), row(COL_GA // D_MODEL)),
                  pl.BlockSpec((tm, D_MODEL), row(COL_GB // D_MODEL)),
                  pl.BlockSpec((tm, CONV_WIDTH), row(COL_CB // CONV_WIDTH)),
                  pl.BlockSpec((tm, CONV_WIDTH), row(COL_CC // CONV_WIDTH)),
                  pl.BlockSpec((tm, CONV_WIDTH), row(COL_CU // CONV_WIDTH)),
                  pl.BlockSpec((tm, D_MODEL), row(0)),
                  pl.BlockSpec((1, 6, D_MODEL), lambda i: (i // tiles_per_seq, 0, 0)),
                  pl.BlockSpec((CONV_K, CONV_WIDTH), const),
                  pl.BlockSpec((1, CONV_WIDTH), const),
                  pl.BlockSpec((CONV_WIDTH, D_MODEL), const),
                  pl.BlockSpec((D_MODEL, D_MODEL), const),
                  pl.BlockSpec((D_MODEL, D_MODEL), const),
                  pl.BlockSpec((1, D_MODEL), const),
                  pl.BlockSpec((D_MODEL, 2 * N_EXPERTS), const),
                  pl.BlockSpec((1, N_EXPERTS), const)],
        out_specs=[pl.BlockSpec((tm, D_MODEL), row(0)),
                   pl.BlockSpec((tm, D_MODEL // 2), row(0)),
                   pl.BlockSpec((tm, TOP_K), row(0)),
                   pl.BlockSpec((tm, TOP_K), row(0)),
                   pl.BlockSpec((1, N_EXPERTS), const)],
        out_shape=[jax.ShapeDtypeStruct((n, D_MODEL), F32),
                   jax.ShapeDtypeStruct((n, D_MODEL // 2), jnp.uint32),
                   jax.ShapeDtypeStruct((n, TOP_K), jnp.int32),
                   jax.ShapeDtypeStruct((n, TOP_K), F32),
                   jax.ShapeDtypeStruct((1, N_EXPERTS), F32)],
        scratch_shapes=[pltpu.VMEM((tm + 8, CONV_WIDTH), F32), pltpu.VMEM((8, CONV_WIDTH), F32)],
        compiler_params=pltpu.CompilerParams(dimension_semantics=("arbitrary",),
                                             vmem_limit_bytes=VMEM_LIMIT),
        name="post",
    )(att, proj, proj, proj, proj, proj, x2, mod3, conv_w, conv_b.reshape(1, -1),
      woc_b, woa_b, wmg_b, norm2_g.reshape(1, -1), wr_split, b_router.reshape(1, -1))


def _scan_kernel(idx_ref, pstart_ref, dest_ref, carry, earlier, *, t):
    @pl.when(pl.program_id(0) == 0)
    def _():
        carry[...] = jnp.zeros(carry.shape, F32)
        r = lax.broadcasted_iota(jnp.int32, (t, t), 0)
        c = lax.broadcasted_iota(jnp.int32, (t, t), 1)
        earlier[...] = jnp.where(c < r, 1.0, 0.0).astype(BF16)

    idx = idx_ref[...]
    lane = lax.broadcasted_iota(jnp.int32, (t, N_EXPERTS), 1)
    hits = [lane == idx[:, k:k + 1] for k in range(TOP_K)]
    member = hits[0].astype(F32)
    for k in range(1, TOP_K):
        member = member + hits[k].astype(F32)
    rank = jnp.dot(earlier[...], member.astype(BF16), preferred_element_type=F32) + carry[...]
    dest_e = pstart_ref[...] + rank
    cols = [jnp.sum(jnp.where(hits[k], dest_e, 0.0), axis=-1, keepdims=True) for k in range(TOP_K)]
    dest_ref[...] = jnp.concatenate(cols, axis=1).astype(jnp.int32)
    carry[...] += jnp.sum(member, axis=0, keepdims=True)


def _scan(idx, pstart):
    n = idx.shape[0]
    t = min(T_SCAN, n)
    return pl.pallas_call(
        functools.partial(_scan_kernel, t=t),
        grid=(n // t,),
        in_specs=[pl.BlockSpec((t, TOP_K), lambda i: (i, 0)),
                  pl.BlockSpec((1, N_EXPERTS), lambda i: (0, 0))],
        out_specs=pl.BlockSpec((t, TOP_K), lambda i: (i, 0)),
        out_shape=jax.ShapeDtypeStruct((n, TOP_K), jnp.int32),
        scratch_shapes=[pltpu.VMEM((1, N_EXPERTS), F32), pltpu.VMEM((t, t), BF16)],
        compiler_params=pltpu.CompilerParams(dimension_semantics=("arbitrary",)),
        name="scan",
    )(idx, pstart)


def _sc_mesh():
    return plsc.VectorSubcoreMesh(core_axis_name="core", subcore_axis_name="subcore")


def _sc_worker(mesh):
    return lax.axis_index("core") * mesh.num_subcores + lax.axis_index("subcore")


def _sc_dispatch(dest_t, hp, rows):
    n, width = hp.shape
    mesh = _sc_mesh()
    workers = mesh.num_cores * mesh.num_subcores
    per_worker = n // (SC_WINDOW * workers)
    assert per_worker * SC_WINDOW * workers == n

    @pl.kernel(out_type=jax.ShapeDtypeStruct((rows, width), hp.dtype), mesh=mesh,
               scratch_types=[pltpu.VMEM((SC_WINDOW, width), hp.dtype),
                              pltpu.VMEM((TOP_K, SC_WINDOW), jnp.int32)])
    def scatter(hp_hbm, dest_hbm, xs_hbm, row_buf, idx_buf):
        first = _sc_worker(mesh) * per_worker

        @pl.loop(0, per_worker)
        def _(w):
            t0 = (first + w) * SC_WINDOW
            pltpu.sync_copy(hp_hbm.at[pl.ds(t0, SC_WINDOW)], row_buf)
            pltpu.sync_copy(dest_hbm.at[:, pl.ds(t0, SC_WINDOW)], idx_buf)
            for k in range(TOP_K):
                pltpu.sync_copy(row_buf, xs_hbm.at[idx_buf.at[k]])

    return scatter(hp, dest_t)


def _sc_gather(src_rows, table):
    n_out = src_rows.shape[1]
    width = table.shape[1]
    mesh = _sc_mesh()
    workers = mesh.num_cores * mesh.num_subcores
    per_worker = n_out // (SC_WINDOW * workers)
    assert per_worker * SC_WINDOW * workers == n_out

    @pl.kernel(out_type=jax.ShapeDtypeStruct((n_out, width), table.dtype), mesh=mesh,
               scratch_types=[pltpu.VMEM((SC_WINDOW, width), table.dtype),
                              pltpu.VMEM((1, SC_WINDOW), jnp.int32)])
    def gather(table_hbm, src_hbm, out_hbm, row_buf, idx_buf):
        first = _sc_worker(mesh) * per_worker

        @pl.loop(0, per_worker)
        def _(w):
            r0 = (first + w) * SC_WINDOW
            pltpu.sync_copy(src_hbm.at[:, pl.ds(r0, SC_WINDOW)], idx_buf)
            pltpu.sync_copy(table_hbm.at[idx_buf.at[0]], row_buf)
            pltpu.sync_copy(row_buf, out_hbm.at[pl.ds(r0, SC_WINDOW)])

    return gather(table, src_rows)


def _expert_kernel(be_ref, first_ref, used_ref, valid_ref, xs_ref, wgu_ref, bg_ref, bl_ref, wd_ref,
                   bd_ref, y_ref, wg_s, wl_s, wd_s, tmp_s):
    del be_ref
    i = pl.program_id(0)

    @pl.when(first_ref[i] == 1)
    def _():
        half = DEINT_CHUNK // 2
        for c in range(wgu_ref.shape[2] // DEINT_CHUNK):
            for db in range(D_MODEL // LANES):
                rows = slice(db * LANES, (db + 1) * LANES)
                tmp_s[...] = wgu_ref[0, rows, c * DEINT_CHUNK:(c + 1) * DEINT_CHUNK].T
                feats = slice(c * half, (c + 1) * half)
                wg_s[feats, rows] = tmp_s[pl.ds(0, half, stride=2), :].astype(BF16)
                wl_s[feats, rows] = tmp_s[pl.ds(1, half, stride=2), :].astype(BF16)
        wd_s[...] = wd_ref[0].astype(BF16)

    @pl.when(i < used_ref[0])
    def _():
        xp = xs_ref[...]
        rid = lax.broadcasted_iota(jnp.int32, xp.shape, 0)
        xp = jnp.where(rid < valid_ref[i], xp, jnp.uint32(0))
        xb = jnp.concatenate(_unpack_halves(xp), axis=1).astype(BF16)
        nt = (((1,), (1,)), ((), ()))
        g = lax.dot_general(xb, wg_s[...], nt, preferred_element_type=F32) + bg_ref[0]
        l = lax.dot_general(xb, wl_s[...], nt, preferred_element_type=F32) + bl_ref[0]
        glu = jnp.minimum(g, SWIGLU_LIMIT)
        lin = jnp.clip(l, -SWIGLU_LIMIT, SWIGLU_LIMIT)
        act = glu * _sigmoid(SWIGLU_ALPHA * glu) * (lin + 1.0)
        y = jnp.dot(act.astype(BF16), wd_s[...], preferred_element_type=F32) + bd_ref[0]
        y_ref[...] = _pack_halves(y)

    @pl.when(i >= used_ref[0])
    def _():
        y_ref[...] = jnp.zeros(y_ref.shape, y_ref.dtype)


def _experts(block_e, first, n_used, valid, xs, wgu, bg, bl, wd, bd):
    rows, width = xs.shape
    d_exp = wd.shape[1]
    n_blocks = rows // ROW_BLOCK
    wsel = lambda i, be, fi, nu, va: (be[i], 0, 0)
    rsel = lambda i, be, fi, nu, va: (jnp.minimum(i, nu[0] - 1), 0)
    grid_spec = pltpu.PrefetchScalarGridSpec(
        num_scalar_prefetch=4,
        grid=(n_blocks,),
        in_specs=[pl.BlockSpec((ROW_BLOCK, width), rsel),
                  pl.BlockSpec((1, D_MODEL, 2 * d_exp), wsel),
                  pl.BlockSpec((1, 1, d_exp), wsel),
                  pl.BlockSpec((1, 1, d_exp), wsel),
                  pl.BlockSpec((1, d_exp, D_MODEL), wsel),
                  pl.BlockSpec((1, 1, D_MODEL), wsel)],
        out_specs=pl.BlockSpec((ROW_BLOCK, width), lambda i, be, fi, nu, va: (i, 0)),
        scratch_shapes=[pltpu.VMEM((d_exp, D_MODEL), BF16), pltpu.VMEM((d_exp, D_MODEL), BF16),
                        pltpu.VMEM((d_exp, D_MODEL), BF16), pltpu.VMEM((DEINT_CHUNK, LANES), F32)],
    )
    return pl.pallas_call(
        _expert_kernel,
        grid_spec=grid_spec,
        out_shape=jax.ShapeDtypeStruct((rows, width), xs.dtype),
        compiler_params=pltpu.CompilerParams(dimension_semantics=("arbitrary",),
                                             vmem_limit_bytes=VMEM_LIMIT),
        name="experts",
    )(block_e, first, n_used, valid, xs, wgu, bg, bl, wd, bd)


def _combine_kernel(y0_ref, y1_ref, y2_ref, y3_ref, x1_ref, tw_ref, mod_ref, gf_ref, *rest):
    o_ref = rest[-1]
    tw = tw_ref[...]
    lo = hi = None
    for k, yk_ref in enumerate((y0_ref, y1_ref, y2_ref, y3_ref)):
        y_lo, y_hi = _unpack_halves(yk_ref[...])
        w = tw[:, k:k + 1]
        lo = w * y_lo if lo is None else lo + w * y_lo
        hi = w * y_hi if hi is None else hi + w * y_hi
    moe = jnp.concatenate([lo, hi], axis=1)
    xo = x1_ref[...] + mod_ref[0, 5:6, :] * moe
    o_ref[...] = _rms(xo, gf_ref[...])


def _combine(yk, x1, tw, mod3, normf_g, seq, tile0, prev):
    n = x1.shape[0]
    t = min(TM_POST, seq)
    tiles_per_seq = seq // t
    n_c = yk.shape[0] // TOP_K
    slot = lambda k: pl.BlockSpec((t, yk.shape[1]), lambda i: (k * (n_c // t) + i, 0))
    tok = lambda w: pl.BlockSpec((t, w), lambda i: (tile0 + i, 0))
    in_specs = [slot(0), slot(1), slot(2), slot(3), tok(D_MODEL), tok(TOP_K),
                pl.BlockSpec((1, 6, D_MODEL), lambda i: ((tile0 + i) // tiles_per_seq, 0, 0)),
                pl.BlockSpec((1, D_MODEL), lambda i: (0, 0))]
    args = [yk, yk, yk, yk, x1, tw, mod3, normf_g.reshape(1, -1)]
    aliases = {}
    if prev is not None:
        aliases = {len(args): 0}
        in_specs.append(pl.BlockSpec(memory_space=pl.ANY))
        args.append(prev)
    return pl.pallas_call(
        _combine_kernel,
        grid=(n_c // t,),
        in_specs=in_specs,
        out_specs=tok(D_MODEL),
        out_shape=jax.ShapeDtypeStruct((n, D_MODEL), F32),
        input_output_aliases=aliases,
        compiler_params=pltpu.CompilerParams(dimension_semantics=("arbitrary",)),
        name="combine",
    )(*args)


def _in_proj_weights(w_in):
    c0 = 3 * CONV_WIDTH
    q, k, v = (w_in[:, c0 + s * D_MODEL:c0 + (s + 1) * D_MODEL] for s in range(3))
    w_nat = jnp.concatenate([k, w_in[:, c0 + 3 * D_MODEL:], w_in[:, :c0]], axis=1).astype(BF16)
    w_tr = jnp.concatenate([q, v], axis=1).T.astype(BF16)
    return w_nat, w_tr


def kernel(x, c, w_ada, b_ada, norm1_g, norm2_g, w_in, conv_w, conv_b, w_out_conv, lambda_q1, lambda_k1, lambda_q2, lambda_k2, subln_g, w_o_attn, w_merge, w_router, b_router, w_gate_up, b_gate_up, w_down, b_down, rel_bias_table, normf_g):
    bsz, seq, _ = x.shape
    n = bsz * seq
    assert w_ada.shape[0] == 1, "single layer"
    assert seq % min(TQ, seq) == 0 and n % TM_POST == 0 and (n * TOP_K) % ROW_BLOCK == 0

    mod, lam = _ada(c, w_ada[0], b_ada[0], lambda_q1[0], lambda_k1[0], lambda_q2[0], lambda_k2[0])
    mod3 = mod.reshape(bsz, 6, D_MODEL)

    proj, proj_t = _inproj(x, mod3, norm1_g[0], *_in_proj_weights(w_in[0]))
    att = _attention(proj, proj_t, lam[0, :1], _bias_tiles(rel_bias_table, min(TQ, seq)), subln_g[0])

    wr = w_router[0]
    wr_hi = wr.astype(BF16)
    wr_lo = (wr - wr_hi.astype(F32)).astype(BF16)
    x1, hp, idx, tw, counts = _post(
        att.reshape(n, -1), proj.reshape(n, -1), x.reshape(n, -1), mod3, conv_w[0], conv_b[0],
        w_out_conv[0].astype(BF16), w_o_attn[0].astype(BF16), w_merge[0].astype(BF16),
        norm2_g[0], jnp.concatenate([wr_hi, wr_lo], axis=1), b_router[0], seq)

    cnt = counts[0].astype(jnp.int32)
    padded = (cnt + ROW_BLOCK - 1) // ROW_BLOCK * ROW_BLOCK
    pend = jnp.cumsum(padded)
    pstart = pend - padded
    n_blocks = n * TOP_K // ROW_BLOCK + N_EXPERTS
    rows = n_blocks * ROW_BLOCK
    block_start = jnp.arange(n_blocks, dtype=jnp.int32) * ROW_BLOCK
    block_e = jnp.minimum(jnp.sum(pend[None, :] <= block_start[:, None], axis=1),
                          N_EXPERTS - 1).astype(jnp.int32)

    dest_t = _scan(idx, pstart.astype(F32).reshape(1, -1)).T
    xs = _sc_dispatch(dest_t, hp, rows)

    n_used = (pend[-1:] // ROW_BLOCK).astype(jnp.int32)
    changed = jnp.concatenate([jnp.ones((1,), jnp.bool_), block_e[1:] != block_e[:-1]])
    first = (changed & (jnp.arange(n_blocks) < n_used[0])).astype(jnp.int32)
    valid = jnp.clip((pstart + cnt)[block_e] - block_start, 0, ROW_BLOCK).astype(jnp.int32)
    bgu = b_gate_up[0]
    y = _experts(block_e, first, n_used, valid, xs, w_gate_up[0], bgu[:, None, 0::2],
                 bgu[:, None, 1::2], w_down[0], b_down[0][:, None, :])

    n_c = n // COMBINE_CHUNKS
    out = None
    for ch in range(COMBINE_CHUNKS):
        src = dest_t[:, ch * n_c:(ch + 1) * n_c].reshape(1, -1)
        out = _combine(_sc_gather(src, y), x1, tw, mod3, normf_g, seq, ch * n_c // min(TM_POST, seq), out)
    return out.reshape(bsz, seq, D_MODEL)
```

```python
import functools
import math

import jax
import jax.numpy as jnp
from jax import lax
from jax.experimental import pallas as pl
from jax.experimental.pallas import tpu as pltpu
from jax.experimental.pallas import tpu_sc as plsc

F32 = jnp.float32
BF16 = jnp.bfloat16

D_MODEL = 1024
CONV_WIDTH = 512
CONV_K = 3
N_HEADS = 8
HEAD_DIM = 64
V_DIM = 2 * HEAD_DIM
N_BUCKETS = 32
MAX_DISTANCE = 128
N_EXPERTS = 32
TOP_K = 4
SWIGLU_ALPHA = 1.702
SWIGLU_LIMIT = 7.0
EPS = 1e-6
LAM_INIT = 0.8 - 0.6 * math.exp(-0.3 * 0)

D_NAT = 3 * CONV_WIDTH + 3 * D_MODEL
COL_K, COL_GA, COL_GB = 0, 1024, 2048
COL_CB, COL_CC, COL_CU = 3072, 3584, 4096
D_TR = 2 * D_MODEL
ROW_QT, ROW_VT = 0, 1024

TM_IN = 1024
TN_IN = 1536
TQ = 512
HEADS_PER_STEP = 2
TM_POST = 512
T_SCAN = 512
SC_WINDOW = 128
COMBINE_CHUNKS = 2
ROW_BLOCK = 512
MASK_NEG = -1e30
ONES_ROWS = 16
LANES = 128
DEINT_CHUNK = 512
VMEM_LIMIT = 56 * 1024 * 1024


def _sigmoid(x):
    return 1.0 / (1.0 + jnp.exp(-x))


def _rms(x, g):
    return x * lax.rsqrt(jnp.mean(x * x, axis=-1, keepdims=True) + EPS) * g


def _pack_halves(v):
    bits = pltpu.bitcast(v.astype(BF16).astype(F32), jnp.uint32)
    w = v.shape[1] // 2
    return (bits[:, w:] & jnp.uint32(0xFFFF0000)) | (bits[:, :w] >> 16)


def _unpack_halves(words):
    return (pltpu.bitcast(words << 16, F32), pltpu.bitcast(words & jnp.uint32(0xFFFF0000), F32))


def _ada_kernel(c_ref, w_ref, b_ref, lq1_ref, lk1_ref, lq2_ref, lk2_ref, mod_ref, lam_ref):
    c = c_ref[...]
    s = c * _sigmoid(c)
    mod_ref[...] = jnp.dot(s, w_ref[...], precision=lax.Precision.HIGHEST,
                           preferred_element_type=F32) + b_ref[...]
    a1 = jnp.sum(lq1_ref[...] * lk1_ref[...], axis=-1, keepdims=True)
    a2 = jnp.sum(lq2_ref[...] * lk2_ref[...], axis=-1, keepdims=True)
    lam_ref[...] = jnp.broadcast_to(jnp.exp(a1) - jnp.exp(a2) + LAM_INIT, lam_ref.shape)


def _ada(c, w_ada, b_ada, lq1, lk1, lq2, lk2):
    bsz = c.shape[0]
    n_chunks = w_ada.shape[1] // D_MODEL
    vec = pl.BlockSpec((1, HEAD_DIM), lambda j: (0, 0))
    return pl.pallas_call(
        _ada_kernel,
        grid=(n_chunks,),
        in_specs=[pl.BlockSpec((bsz, D_MODEL), lambda j: (0, 0)),
                  pl.BlockSpec((D_MODEL, D_MODEL), lambda j: (0, j)),
                  pl.BlockSpec((1, D_MODEL), lambda j: (0, j)),
                  vec, vec, vec, vec],
        out_specs=[pl.BlockSpec((bsz, D_MODEL), lambda j: (0, j)),
                   pl.BlockSpec((1, 128), lambda j: (0, 0))],
        out_shape=[jax.ShapeDtypeStruct((bsz, n_chunks * D_MODEL), F32),
                   jax.ShapeDtypeStruct((1, 128), F32)],
        name="ada",
    )(c, w_ada, b_ada.reshape(1, -1), lq1.reshape(1, -1), lk1.reshape(1, -1),
      lq2.reshape(1, -1), lk2.reshape(1, -1))


def _inproj_kernel(x_ref, mod_ref, g_ref, w_ref, wt_ref, o_ref, ot_ref, h_scr, ht_scr, *, n_nat):
    j = pl.program_id(2)

    @pl.when(j == 0)
    def _():
        h = _rms(x_ref[0], g_ref[...]) * (1.0 + mod_ref[0, 1:2, :]) + mod_ref[0, 0:1, :]
        h_scr[...] = h.astype(BF16)
        ht_scr[...] = h.T.astype(BF16)

    @pl.when(j < n_nat)
    def _():
        o_ref[0] = jnp.dot(h_scr[...], w_ref[...], preferred_element_type=F32).astype(BF16)

    @pl.when(j == n_nat)
    def _():
        ot_ref[0] = jnp.dot(wt_ref[...], ht_scr[...], preferred_element_type=F32).astype(BF16)


def _inproj(x, mod3, norm1_g, w_nat, w_tr):
    bsz, seq, _ = x.shape
    tm = min(TM_IN, seq)
    n_nat = D_NAT // TN_IN
    nat_j = lambda j: jnp.minimum(j, n_nat - 1)
    return pl.pallas_call(
        functools.partial(_inproj_kernel, n_nat=n_nat),
        grid=(bsz, seq // tm, n_nat + 1),
        in_specs=[pl.BlockSpec((1, tm, D_MODEL), lambda b, i, j: (b, i, 0)),
                  pl.BlockSpec((1, 6, D_MODEL), lambda b, i, j: (b, 0, 0)),
                  pl.BlockSpec((1, D_MODEL), lambda b, i, j: (0, 0)),
                  pl.BlockSpec((D_MODEL, TN_IN), lambda b, i, j: (0, nat_j(j))),
                  pl.BlockSpec((D_TR, D_MODEL), lambda b, i, j: (0, 0))],
        out_specs=[pl.BlockSpec((1, tm, TN_IN), lambda b, i, j: (b, i, nat_j(j))),
                   pl.BlockSpec((1, D_TR, tm), lambda b, i, j: (b, 0, i))],
        out_shape=[jax.ShapeDtypeStruct((bsz, seq, D_NAT), BF16),
                   jax.ShapeDtypeStruct((bsz, D_TR, seq), BF16)],
        scratch_shapes=[pltpu.VMEM((tm, D_MODEL), BF16), pltpu.VMEM((D_MODEL, tm), BF16)],
        compiler_params=pltpu.CompilerParams(
            dimension_semantics=("arbitrary", "arbitrary", "arbitrary"),
            vmem_limit_bytes=VMEM_LIMIT),
        name="inproj",
    )(x, mod3, norm1_g.reshape(1, -1), w_nat, w_tr)


def _attn_kernel(lam_ref, qt_ref, k_ref, vt_ref, bias_ref, g_ref, o_ref,
                 m_scr, acc_scr, vte_scr, s_a, s_b, t_a, t_b, *, tq):
    i = pl.program_id(2)
    heads = range(HEADS_PER_STEP)
    n_chain = 2 * HEADS_PER_STEP
    head_rows = lambda hh: slice(hh * V_DIM, (hh + 1) * V_DIM)
    sub_cols = lambda sub: slice(sub * tq, (sub + 1) * tq)

    @pl.when(i == 0)
    def _():
        for hh in heads:
            vte_scr[hh, 0:V_DIM, :] = vt_ref[0, head_rows(hh), :]
            vte_scr[hh, V_DIM:, :] = jnp.ones((ONES_ROWS, vte_scr.shape[2]), BF16)

    qt_maps = []
    for sub in range(2):
        maps = []
        for hh in heads:
            qt = qt_ref[0, head_rows(hh), sub_cols(sub)] * jnp.asarray(HEAD_DIM ** -0.5, BF16)
            row = lax.broadcasted_iota(jnp.int32, qt.shape, 0)
            zero = jnp.zeros_like(qt)
            maps += [jnp.where(row < HEAD_DIM, qt, zero), jnp.where(row >= HEAD_DIM, qt, zero)]
        qt_maps.append(maps)

    m_scr[...] = jnp.full(m_scr.shape, MASK_NEG, F32)
    acc_scr[...] = jnp.zeros(acc_scr.shape, F32)
    slots = ((s_a, t_a), (s_b, t_b))

    def produce(sub, j, bias, slot):
        s_ref, t_ref = slots[slot]
        for hh in heads:
            k = k_ref[0, pl.ds(pl.multiple_of(j * tq, tq), tq), head_rows(hh)]
            for m in range(2):
                c = 2 * hh + m
                s = jnp.dot(k, qt_maps[sub][c], preferred_element_type=F32)
                if bias is not None:
                    s = s + bias_ref[hh, bias]
                s_ref[c] = s
                t_ref[c] = jnp.max(s, axis=0, keepdims=True)

    def consume(sub, j, slot):
        s_ref, t_ref = slots[slot]
        for hh in heads:
            vte = vte_scr[hh, :, pl.ds(pl.multiple_of(j * tq, tq), tq)]
            for m in range(2):
                c = 2 * hh + m
                a = sub * n_chain + c
                m_prev = m_scr[a]
                m_new = jnp.maximum(m_prev, t_ref[c])
                alpha = jnp.exp(m_prev - m_new)
                p = jnp.exp(s_ref[c] - m_new).astype(BF16)
                acc_scr[a] = alpha * acc_scr[a] + jnp.dot(vte, p, preferred_element_type=F32)
                m_scr[a] = m_new

    diag, off1 = 0, 1
    qa, qb = 0, 1

    @pl.when(i == 0)
    def _():
        produce(qa, 0, diag, 0)
        produce(qb, 0, off1, 1)
        consume(qa, 0, 0)
        produce(qb, 1, diag, 0)
        consume(qb, 0, 1)
        consume(qb, 1, 0)

    @pl.when(i >= 1)
    def _():
        ia = 2 * i
        ib = ia + 1
        produce(qa, 0, None, 0)

        def pair_a(r, carry):
            t = 2 * r
            produce(qa, t + 1, None, 1)
            consume(qa, t, 0)
            produce(qa, t + 2, None, 0)
            consume(qa, t + 1, 1)
            return carry

        lax.fori_loop(0, i - 1, pair_a, 0)
        produce(qa, ia - 1, off1, 1)
        consume(qa, ia - 2, 0)
        produce(qa, ia, diag, 0)
        consume(qa, ia - 1, 1)
        produce(qb, 0, None, 1)
        consume(qa, ia, 0)

        def pair_b(r, carry):
            t = 2 * r
            produce(qb, t + 1, None, 0)
            consume(qb, t, 1)
            produce(qb, t + 2, None, 1)
            consume(qb, t + 1, 0)
            return carry

        lax.fori_loop(0, i - 1, pair_b, 0)
        produce(qb, ib - 2, None, 0)
        consume(qb, ib - 3, 1)
        produce(qb, ib - 1, off1, 1)
        consume(qb, ib - 2, 0)
        produce(qb, ib, diag, 0)
        consume(qb, ib - 1, 1)
        consume(qb, ib, 0)

    lam = lam_ref[0]
    for sub in range(2):
        for hh in heads:
            a1, a2 = acc_scr[sub * n_chain + 2 * hh], acc_scr[sub * n_chain + 2 * hh + 1]
            ot = a1[:V_DIM] / a1[V_DIM:V_DIM + 1] - lam * (a2[:V_DIM] / a2[V_DIM:V_DIM + 1])
            ot = ot * lax.rsqrt(jnp.mean(ot * ot, axis=0, keepdims=True) + EPS) * g_ref[...]
            o_ref[0, sub_cols(sub), head_rows(hh)] = (ot * (1.0 - LAM_INIT)).T.astype(BF16)


def _attention(proj, proj_t, lam, bias_tiles, subln_g):
    bsz, seq, _ = proj.shape
    tq = min(TQ, seq // 2)
    n_pairs = seq // (2 * tq)
    hps = HEADS_PER_STEP
    width = hps * V_DIM
    kern = functools.partial(_attn_kernel, tq=tq)
    score = pltpu.VMEM((2 * hps, tq, tq), F32)
    colmax = pltpu.VMEM((2 * hps, 1, tq), F32)
    return pl.pallas_call(
        kern,
        grid=(bsz, N_HEADS // hps, n_pairs),
        in_specs=[pl.BlockSpec(memory_space=pltpu.SMEM),
                  pl.BlockSpec((1, width, 2 * tq), lambda b, h, i: (b, ROW_QT // width + h, i)),
                  pl.BlockSpec((1, seq, width), lambda b, h, i: (b, 0, COL_K // width + h)),
                  pl.BlockSpec((1, width, seq), lambda b, h, i: (b, ROW_VT // width + h, 0)),
                  pl.BlockSpec((hps, 2, tq, tq), lambda b, h, i: (h, 0, 0, 0)),
                  pl.BlockSpec((V_DIM, 1), lambda b, h, i: (0, 0))],
        out_specs=pl.BlockSpec((1, 2 * tq, width), lambda b, h, i: (b, i, h)),
        out_shape=jax.ShapeDtypeStruct((bsz, seq, N_HEADS * V_DIM), BF16),
        scratch_shapes=[pltpu.VMEM((4 * hps, 1, tq), F32),
                        pltpu.VMEM((4 * hps, V_DIM + ONES_ROWS, tq), F32),
                        pltpu.VMEM((hps, V_DIM + ONES_ROWS, seq), BF16),
                        score, score, colmax, colmax],
        compiler_params=pltpu.CompilerParams(
            dimension_semantics=("arbitrary", "arbitrary", "arbitrary"),
            vmem_limit_bytes=VMEM_LIMIT),
        name="attn",
    )(lam, proj_t, proj, proj_t, bias_tiles, subln_g.reshape(-1, 1))


def _t5_bucket(dist):
    max_exact = N_BUCKETS // 2
    d = jnp.maximum(dist, 1).astype(F32)
    large = max_exact + (jnp.log(d / max_exact) / math.log(MAX_DISTANCE / max_exact)
                         * (N_BUCKETS - max_exact)).astype(jnp.int32)
    large = jnp.minimum(large, N_BUCKETS - 1)
    return jnp.where(dist < max_exact, dist, large)


def _bias_tiles(rel_table, tq):
    assert tq >= MAX_DISTANCE
    r = jnp.arange(tq, dtype=jnp.int32)
    rel0 = r[None, :] - r[:, None]
    rel = jnp.stack([rel0, rel0 + tq])
    table = (rel_table - rel_table[N_BUCKETS - 1:N_BUCKETS, :]).astype(F32)
    bucket = _t5_bucket(jnp.maximum(rel, 0))
    onehot = (bucket[None] == jnp.arange(N_BUCKETS, dtype=jnp.int32)[:, None, None, None]).astype(F32)
    bias = jnp.einsum('kh,kabc->habc', table, onehot, precision=lax.Precision.HIGHEST)
    return jnp.where((rel >= 0)[None], bias, MASK_NEG)


def _post_kernel(att_ref, ga_ref, gb_ref, cb_ref, cc_ref, cu_ref, x_ref, mod_ref,
                 cw_ref, cbias_ref, woc_ref, woa_ref, wmg_ref, g2_ref, wr_ref, br_ref,
                 x1_ref, hp_ref, idx_ref, tw_ref, cnt_ref,
                 ubuf, carry, *, tm, tiles_per_seq):
    i = pl.program_id(0)

    @pl.when(i % tiles_per_seq == 0)
    def _():
        carry[...] = jnp.zeros(carry.shape, F32)

    @pl.when(i == 0)
    def _():
        cnt_ref[...] = jnp.zeros(cnt_ref.shape, F32)

    u = cc_ref[...].astype(F32) * cu_ref[...].astype(F32)
    ubuf[0:8, :] = carry[...]
    ubuf[8:8 + tm, :] = u
    carry[...] = u[tm - 8:tm, :]
    conv = (cbias_ref[...] + ubuf[6:6 + tm, :] * cw_ref[0:1, :]
            + ubuf[7:7 + tm, :] * cw_ref[1:2, :] + u * cw_ref[2:3, :])
    ya = jnp.dot((cb_ref[...].astype(F32) * conv).astype(BF16), woc_ref[...],
                 preferred_element_type=F32)
    yb = jnp.dot(att_ref[...], woa_ref[...], preferred_element_type=F32)
    merged = _sigmoid(ga_ref[...].astype(F32)) * ya + _sigmoid(gb_ref[...].astype(F32)) * yb
    mix = jnp.dot(merged.astype(BF16), wmg_ref[...], preferred_element_type=F32)
    x1 = x_ref[...] + mod_ref[0, 2:3, :] * mix
    x1_ref[...] = x1

    h2 = _rms(x1, g2_ref[...]) * (1.0 + mod_ref[0, 4:5, :]) + mod_ref[0, 3:4, :]
    h_hi = h2.astype(BF16)
    h_hi32 = h_hi.astype(F32)
    h_lo = (h2 - h_hi32).astype(BF16)

    part = jnp.dot(h_hi, wr_ref[...], preferred_element_type=F32)
    logits = (part[:, :N_EXPERTS] + part[:, N_EXPERTS:]
              + jnp.dot(h_lo, wr_ref[:, :N_EXPERTS], preferred_element_type=F32)
              + br_ref[...])

    lane = lax.broadcasted_iota(jnp.int32, logits.shape, 1)
    work = logits
    sel_l, sel_i = [], []
    member = jnp.zeros(logits.shape, F32)
    for _ in range(TOP_K):
        mx = jnp.max(work, axis=-1, keepdims=True)
        ix = jnp.min(jnp.where(work == mx, lane, N_EXPERTS), axis=-1, keepdims=True)
        hit = lane == ix
        member = member + hit.astype(F32)
        work = jnp.where(hit, -jnp.inf, work)
        sel_l.append(mx)
        sel_i.append(ix)
    ex = [jnp.exp(l - sel_l[0]) for l in sel_l]
    den = ex[0] + ex[1] + ex[2] + ex[3]
    tw_ref[...] = jnp.concatenate([e / den for e in ex], axis=1)
    idx_ref[...] = jnp.concatenate(sel_i, axis=1)
    cnt_ref[...] += jnp.sum(member, axis=0, keepdims=True)

    hp_ref[...] = _pack_halves(h2)


def _post(att, proj, x2, mod3, conv_w, conv_b, woc_b, woa_b, wmg_b, norm2_g, wr_split, b_router, seq):
    n = x2.shape[0]
    tm = min(TM_POST, seq)
    tiles_per_seq = seq // tm
    kern = functools.partial(_post_kernel, tm=tm, tiles_per_seq=tiles_per_seq)
    row = lambda c: (lambda i: (i, c))
    const = lambda i: (0, 0)
    return pl.pallas_call(
        kern,
        grid=(n // tm,),
        in_specs=[pl.BlockSpec((tm, D_MODEL), row(0)),
                  pl.BlockSpec((tm, D_MODEL), row(COL_GA // D_MODEL)),
                  pl.BlockSpec((tm, D_MODEL), row(COL_GB // D_MODEL)),
                  pl.BlockSpec((tm, CONV_WIDTH), row(COL_CB // CONV_WIDTH)),
                  pl.BlockSpec((tm, CONV_WIDTH), row(COL_CC // CONV_WIDTH)),
                  pl.BlockSpec((tm, CONV_WIDTH), row(COL_CU // CONV_WIDTH)),
                  pl.BlockSpec((tm, D_MODEL), row(0)),
                  pl.BlockSpec((1, 6, D_MODEL), lambda i: (i // tiles_per_seq, 0, 0)),
                  pl.BlockSpec((CONV_K, CONV_WIDTH), const),
                  pl.BlockSpec((1, CONV_WIDTH), const),
                  pl.BlockSpec((CONV_WIDTH, D_MODEL), const),
                  pl.BlockSpec((D_MODEL, D_MODEL), const),
                  pl.BlockSpec((D_MODEL, D_MODEL), const),
                  pl.BlockSpec((1, D_MODEL), const),
                  pl.BlockSpec((D_MODEL, 2 * N_EXPERTS), const),
                  pl.BlockSpec((1, N_EXPERTS), const)],
        out_specs=[pl.BlockSpec((tm, D_MODEL), row(0)),
                   pl.BlockSpec((tm, D_MODEL // 2), row(0)),
                   pl.BlockSpec((tm, TOP_K), row(0)),
                   pl.BlockSpec((tm, TOP_K), row(0)),
                   pl.BlockSpec((1, N_EXPERTS), const)],
        out_shape=[jax.ShapeDtypeStruct((n, D_MODEL), F32),
                   jax.ShapeDtypeStruct((n, D_MODEL // 2), jnp.uint32),
                   jax.ShapeDtypeStruct((n, TOP_K), jnp.int32),
                   jax.ShapeDtypeStruct((n, TOP_K), F32),
                   jax.ShapeDtypeStruct((1, N_EXPERTS), F32)],
        scratch_shapes=[pltpu.VMEM((tm + 8, CONV_WIDTH), F32), pltpu.VMEM((8, CONV_WIDTH), F32)],
        compiler_params=pltpu.CompilerParams(dimension_semantics=("arbitrary",),
                                             vmem_limit_bytes=VMEM_LIMIT),
        name="post",
    )(att, proj, proj, proj, proj, proj, x2, mod3, conv_w, conv_b.reshape(1, -1),
      woc_b, woa_b, wmg_b, norm2_g.reshape(1, -1), wr_split, b_router.reshape(1, -1))


def _scan_kernel(idx_ref, pstart_ref, dest_ref, carry, earlier, *, t):
    @pl.when(pl.program_id(0) == 0)
    def _():
        carry[...] = jnp.zeros(carry.shape, F32)
        r = lax.broadcasted_iota(jnp.int32, (t, t), 0)
        c = lax.broadcasted_iota(jnp.int32, (t, t), 1)
        earlier[...] = jnp.where(c < r, 1.0, 0.0).astype(BF16)

    idx = idx_ref[...]
    lane = lax.broadcasted_iota(jnp.int32, (t, N_EXPERTS), 1)
    hits = [lane == idx[:, k:k + 1] for k in range(TOP_K)]
    member = hits[0].astype(F32)
    for k in range(1, TOP_K):
        member = member + hits[k].astype(F32)
    rank = jnp.dot(earlier[...], member.astype(BF16), preferred_element_type=F32) + carry[...]
    dest_e = pstart_ref[...] + rank
    cols = [jnp.sum(jnp.where(hits[k], dest_e, 0.0), axis=-1, keepdims=True) for k in range(TOP_K)]
    dest_ref[...] = jnp.concatenate(cols, axis=1).astype(jnp.int32)
    carry[...] += jnp.sum(member, axis=0, keepdims=True)


def _scan(idx, pstart):
    n = idx.shape[0]
    t = min(T_SCAN, n)
    return pl.pallas_call(
        functools.partial(_scan_kernel, t=t),
        grid=(n // t,),
        in_specs=[pl.BlockSpec((t, TOP_K), lambda i: (i, 0)),
                  pl.BlockSpec((1, N_EXPERTS), lambda i: (0, 0))],
        out_specs=pl.BlockSpec((t, TOP_K), lambda i: (i, 0)),
        out_shape=jax.ShapeDtypeStruct((n, TOP_K), jnp.int32),
        scratch_shapes=[pltpu.VMEM((1, N_EXPERTS), F32), pltpu.VMEM((t, t), BF16)],
        compiler_params=pltpu.CompilerParams(dimension_semantics=("arbitrary",)),
        name="scan",
    )(idx, pstart)


def _sc_mesh():
    return plsc.VectorSubcoreMesh(core_axis_name="core", subcore_axis_name="subcore")


def _sc_worker(mesh):
    return lax.axis_index("core") * mesh.num_subcores + lax.axis_index("subcore")


def _sc_dispatch(dest_t, hp, rows):
    n, width = hp.shape
    mesh = _sc_mesh()
    workers = mesh.num_cores * mesh.num_subcores
    per_worker = n // (SC_WINDOW * workers)
    assert per_worker * SC_WINDOW * workers == n

    @pl.kernel(out_type=jax.ShapeDtypeStruct((rows, width), hp.dtype), mesh=mesh,
               scratch_types=[pltpu.VMEM((SC_WINDOW, width), hp.dtype),
                              pltpu.VMEM((TOP_K, SC_WINDOW), jnp.int32)])
    def scatter(hp_hbm, dest_hbm, xs_hbm, row_buf, idx_buf):
        first = _sc_worker(mesh) * per_worker

        @pl.loop(0, per_worker)
        def _(w):
            t0 = (first + w) * SC_WINDOW
            pltpu.sync_copy(hp_hbm.at[pl.ds(t0, SC_WINDOW)], row_buf)
            pltpu.sync_copy(dest_hbm.at[:, pl.ds(t0, SC_WINDOW)], idx_buf)
            for k in range(TOP_K):
                pltpu.sync_copy(row_buf, xs_hbm.at[idx_buf.at[k]])

    return scatter(hp, dest_t)


def _sc_gather(src_rows, table):
    n_out = src_rows.shape[1]
    width = table.shape[1]
    mesh = _sc_mesh()
    workers = mesh.num_cores * mesh.num_subcores
    per_worker = n_out // (SC_WINDOW * workers)
    assert per_worker * SC_WINDOW * workers == n_out

    @pl.kernel(out_type=jax.ShapeDtypeStruct((n_out, width), table.dtype), mesh=mesh,
               scratch_types=[pltpu.VMEM((SC_WINDOW, width), table.dtype),
                              pltpu.VMEM((1, SC_WINDOW), jnp.int32)])
    def gather(table_hbm, src_hbm, out_hbm, row_buf, idx_buf):
        first = _sc_worker(mesh) * per_worker

        @pl.loop(0, per_worker)
        def _(w):
            r0 = (first + w) * SC_WINDOW
            pltpu.sync_copy(src_hbm.at[:, pl.ds(r0, SC_WINDOW)], idx_buf)
            pltpu.sync_copy(table_hbm.at[idx_buf.at[0]], row_buf)
            pltpu.sync_copy(row_buf, out_hbm.at[pl.ds(r0, SC_WINDOW)])

    return gather(table, src_rows)


def _expert_kernel(be_ref, first_ref, used_ref, valid_ref, xs_ref, wgu_ref, bg_ref, bl_ref, wd_ref,
                   bd_ref, y_ref, wg_s, wl_s, wd_s, tmp_s):
    del be_ref
    i = pl.program_id(0)

    @pl.when(first_ref[i] == 1)
    def _():
        half = DEINT_CHUNK // 2
        for c in range(wgu_ref.shape[2] // DEINT_CHUNK):
            for db in range(D_MODEL // LANES):
                rows = slice(db * LANES, (db + 1) * LANES)
                tmp_s[...] = wgu_ref[0, rows, c * DEINT_CHUNK:(c + 1) * DEINT_CHUNK].T
                feats = slice(c * half, (c + 1) * half)
                wg_s[feats, rows] = tmp_s[pl.ds(0, half, stride=2), :].astype(BF16)
                wl_s[feats, rows] = tmp_s[pl.ds(1, half, stride=2), :].astype(BF16)
        wd_s[...] = wd_ref[0].astype(BF16)

    @pl.when(i < used_ref[0])
    def _():
        xp = xs_ref[...]
        rid = lax.broadcasted_iota(jnp.int32, xp.shape, 0)
        xp = jnp.where(rid < valid_ref[i], xp, jnp.uint32(0))
        xb = jnp.concatenate(_unpack_halves(xp), axis=1).astype(BF16)
        nt = (((1,), (1,)), ((), ()))
        g = lax.dot_general(xb, wg_s[...], nt, preferred_element_type=F32) + bg_ref[0]
        l = lax.dot_general(xb, wl_s[...], nt, preferred_element_type=F32) + bl_ref[0]
        glu = jnp.minimum(g, SWIGLU_LIMIT)
        lin = jnp.clip(l, -SWIGLU_LIMIT, SWIGLU_LIMIT)
        act = glu * _sigmoid(SWIGLU_ALPHA * glu) * (lin + 1.0)
        y = jnp.dot(act.astype(BF16), wd_s[...], preferred_element_type=F32) + bd_ref[0]
        y_ref[...] = _pack_halves(y)

    @pl.when(i >= used_ref[0])
    def _():
        y_ref[...] = jnp.zeros(y_ref.shape, y_ref.dtype)


def _experts(block_e, first, n_used, valid, xs, wgu, bg, bl, wd, bd):
    rows, width = xs.shape
    d_exp = wd.shape[1]
    n_blocks = rows // ROW_BLOCK
    wsel = lambda i, be, fi, nu, va: (be[i], 0, 0)
    rsel = lambda i, be, fi, nu, va: (jnp.minimum(i, nu[0] - 1), 0)
    grid_spec = pltpu.PrefetchScalarGridSpec(
        num_scalar_prefetch=4,
        grid=(n_blocks,),
        in_specs=[pl.BlockSpec((ROW_BLOCK, width), rsel),
                  pl.BlockSpec((1, D_MODEL, 2 * d_exp), wsel),
                  pl.BlockSpec((1, 1, d_exp), wsel),
                  pl.BlockSpec((1, 1, d_exp), wsel),
                  pl.BlockSpec((1, d_exp, D_MODEL), wsel),
                  pl.BlockSpec((1, 1, D_MODEL), wsel)],
        out_specs=pl.BlockSpec((ROW_BLOCK, width), lambda i, be, fi, nu, va: (i, 0)),
        scratch_shapes=[pltpu.VMEM((d_exp, D_MODEL), BF16), pltpu.VMEM((d_exp, D_MODEL), BF16),
                        pltpu.VMEM((d_exp, D_MODEL), BF16), pltpu.VMEM((DEINT_CHUNK, LANES), F32)],
    )
    return pl.pallas_call(
        _expert_kernel,
        grid_spec=grid_spec,
        out_shape=jax.ShapeDtypeStruct((rows, width), xs.dtype),
        compiler_params=pltpu.CompilerParams(dimension_semantics=("arbitrary",),
                                             vmem_limit_bytes=VMEM_LIMIT),
        name="experts",
    )(block_e, first, n_used, valid, xs, wgu, bg, bl, wd, bd)


def _combine_kernel(y0_ref, y1_ref, y2_ref, y3_ref, x1_ref, tw_ref, mod_ref, gf_ref, *rest):
    o_ref = rest[-1]
    tw = tw_ref[...]
    lo = hi = None
    for k, yk_ref in enumerate((y0_ref, y1_ref, y2_ref, y3_ref)):
        y_lo, y_hi = _unpack_halves(yk_ref[...])
        w = tw[:, k:k + 1]
        lo = w * y_lo if lo is None else lo + w * y_lo
        hi = w * y_hi if hi is None else hi + w * y_hi
    moe = jnp.concatenate([lo, hi], axis=1)
    xo = x1_ref[...] + mod_ref[0, 5:6, :] * moe
    o_ref[...] = _rms(xo, gf_ref[...])


def _combine(yk, x1, tw, mod3, normf_g, seq, tile0, prev):
    n = x1.shape[0]
    t = min(TM_POST, seq)
    tiles_per_seq = seq // t
    n_c = yk.shape[0] // TOP_K
    slot = lambda k: pl.BlockSpec((t, yk.shape[1]), lambda i: (k * (n_c // t) + i, 0))
    tok = lambda w: pl.BlockSpec((t, w), lambda i: (tile0 + i, 0))
    in_specs = [slot(0), slot(1), slot(2), slot(3), tok(D_MODEL), tok(TOP_K),
                pl.BlockSpec((1, 6, D_MODEL), lambda i: ((tile0 + i) // tiles_per_seq, 0, 0)),
                pl.BlockSpec((1, D_MODEL), lambda i: (0, 0))]
    args = [yk, yk, yk, yk, x1, tw, mod3, normf_g.reshape(1, -1)]
    aliases = {}
    if prev is not None:
        aliases = {len(args): 0}
        in_specs.append(pl.BlockSpec(memory_space=pl.ANY))
        args.append(prev)
    return pl.pallas_call(
        _combine_kernel,
        grid=(n_c // t,),
        in_specs=in_specs,
        out_specs=tok(D_MODEL),
        out_shape=jax.ShapeDtypeStruct((n, D_MODEL), F32),
        input_output_aliases=aliases,
        compiler_params=pltpu.CompilerParams(dimension_semantics=("arbitrary",)),
        name="combine",
    )(*args)


def _in_proj_weights(w_in):
    c0 = 3 * CONV_WIDTH
    q, k, v = (w_in[:, c0 + s * D_MODEL:c0 + (s + 1) * D_MODEL] for s in range(3))
    w_nat = jnp.concatenate([k, w_in[:, c0 + 3 * D_MODEL:], w_in[:, :c0]], axis=1).astype(BF16)
    w_tr = jnp.concatenate([q, v], axis=1).T.astype(BF16)
    return w_nat, w_tr


def kernel(x, c, w_ada, b_ada, norm1_g, norm2_g, w_in, conv_w, conv_b, w_out_conv, lambda_q1, lambda_k1, lambda_q2, lambda_k2, subln_g, w_o_attn, w_merge, w_router, b_router, w_gate_up, b_gate_up, w_down, b_down, rel_bias_table, normf_g):
    bsz, seq, _ = x.shape
    n = bsz * seq
    assert w_ada.shape[0] == 1, "single layer"
    assert seq % (2 * min(TQ, seq // 2)) == 0 and n % TM_POST == 0 and (n * TOP_K) % ROW_BLOCK == 0

    mod, lam = _ada(c, w_ada[0], b_ada[0], lambda_q1[0], lambda_k1[0], lambda_q2[0], lambda_k2[0])
    mod3 = mod.reshape(bsz, 6, D_MODEL)

    proj, proj_t = _inproj(x, mod3, norm1_g[0], *_in_proj_weights(w_in[0]))
    att = _attention(proj, proj_t, lam[0, :1], _bias_tiles(rel_bias_table, min(TQ, seq // 2)), subln_g[0])

    wr = w_router[0]
    wr_hi = wr.astype(BF16)
    wr_lo = (wr - wr_hi.astype(F32)).astype(BF16)
    x1, hp, idx, tw, counts = _post(
        att.reshape(n, -1), proj.reshape(n, -1), x.reshape(n, -1), mod3, conv_w[0], conv_b[0],
        w_out_conv[0].astype(BF16), w_o_attn[0].astype(BF16), w_merge[0].astype(BF16),
        norm2_g[0], jnp.concatenate([wr_hi, wr_lo], axis=1), b_router[0], seq)

    cnt = counts[0].astype(jnp.int32)
    padded = (cnt + ROW_BLOCK - 1) // ROW_BLOCK * ROW_BLOCK
    pend = jnp.cumsum(padded)
    pstart = pend - padded
    n_blocks = n * TOP_K // ROW_BLOCK + N_EXPERTS
    rows = n_blocks * ROW_BLOCK
    block_start = jnp.arange(n_blocks, dtype=jnp.int32) * ROW_BLOCK
    block_e = jnp.minimum(jnp.sum(pend[None, :] <= block_start[:, None], axis=1),
                          N_EXPERTS - 1).astype(jnp.int32)

    dest_t = _scan(idx, pstart.astype(F32).reshape(1, -1)).T
    xs = _sc_dispatch(dest_t, hp, rows)

    n_used = (pend[-1:] // ROW_BLOCK).astype(jnp.int32)
    changed = jnp.concatenate([jnp.ones((1,), jnp.bool_), block_e[1:] != block_e[:-1]])
    first = (changed & (jnp.arange(n_blocks) < n_used[0])).astype(jnp.int32)
    valid = jnp.clip((pstart + cnt)[block_e] - block_start, 0, ROW_BLOCK).astype(jnp.int32)
    bgu = b_gate_up[0]
    y = _experts(block_e, first, n_used, valid, xs, w_gate_up[0], bgu[:, None, 0::2],
                 bgu[:, None, 1::2], w_down[0], b_down[0][:, None, :])

    n_c = n // COMBINE_CHUNKS
    out = None
    for ch in range(COMBINE_CHUNKS):
        src = dest_t[:, ch * n_c:(ch + 1) * n_c].reshape(1, -1)
        out = _combine(_sc_gather(src, y), x1, tw, mod3, normf_g, seq, ch * n_c // min(TM_POST, seq), out)
    return out.reshape(bsz, seq, D_MODEL)
```

```python
import functools
import math

import jax
import jax.numpy as jnp
from jax import lax
from jax.experimental import pallas as pl
from jax.experimental.pallas import tpu as pltpu
from jax.experimental.pallas import tpu_sc as plsc

F32 = jnp.float32
BF16 = jnp.bfloat16

D_MODEL = 1024
CONV_WIDTH = 512
CONV_K = 3
N_HEADS = 8
HEAD_DIM = 64
V_DIM = 2 * HEAD_DIM
N_BUCKETS = 32
MAX_DISTANCE = 128
N_EXPERTS = 32
TOP_K = 4
SWIGLU_ALPHA = 1.702
SWIGLU_LIMIT = 7.0
EPS = 1e-6
LAM_INIT = 0.8 - 0.6 * math.exp(-0.3 * 0)

D_NAT = 3 * CONV_WIDTH + 3 * D_MODEL
COL_K, COL_GA, COL_GB = 0, 1024, 2048
COL_CB, COL_CC, COL_CU = 3072, 3584, 4096
D_TR = 2 * D_MODEL
ROW_QT, ROW_VT = 0, 1024

TM_IN = 1024
TN_IN = 1536
TQ = 512
HEADS_PER_STEP = 2
Q_TILES_PER_STEP = 4
TM_POST = 512
T_SCAN = 512
SC_WINDOW = 128
COMBINE_CHUNKS = 2
ROW_BLOCK = 512
MASK_NEG = -1e30
ONES_ROWS = 16
LANES = 128
DEINT_CHUNK = 512
VMEM_LIMIT = 56 * 1024 * 1024


def _sigmoid(x):
    return 1.0 / (1.0 + jnp.exp(-x))


def _rms(x, g):
    return x * lax.rsqrt(jnp.mean(x * x, axis=-1, keepdims=True) + EPS) * g


def _pack_halves(v):
    bits = pltpu.bitcast(v.astype(BF16).astype(F32), jnp.uint32)
    w = v.shape[1] // 2
    return (bits[:, w:] & jnp.uint32(0xFFFF0000)) | (bits[:, :w] >> 16)


def _unpack_halves(words):
    return (pltpu.bitcast(words << 16, F32), pltpu.bitcast(words & jnp.uint32(0xFFFF0000), F32))


def _ada_kernel(c_ref, w_ref, b_ref, lq1_ref, lk1_ref, lq2_ref, lk2_ref, mod_ref, lam_ref):
    c = c_ref[...]
    s = c * _sigmoid(c)
    mod_ref[...] = jnp.dot(s, w_ref[...], precision=lax.Precision.HIGHEST,
                           preferred_element_type=F32) + b_ref[...]
    a1 = jnp.sum(lq1_ref[...] * lk1_ref[...], axis=-1, keepdims=True)
    a2 = jnp.sum(lq2_ref[...] * lk2_ref[...], axis=-1, keepdims=True)
    lam_ref[...] = jnp.broadcast_to(jnp.exp(a1) - jnp.exp(a2) + LAM_INIT, lam_ref.shape)


def _ada(c, w_ada, b_ada, lq1, lk1, lq2, lk2):
    bsz = c.shape[0]
    n_chunks = w_ada.shape[1] // D_MODEL
    vec = pl.BlockSpec((1, HEAD_DIM), lambda j: (0, 0))
    return pl.pallas_call(
        _ada_kernel,
        grid=(n_chunks,),
        in_specs=[pl.BlockSpec((bsz, D_MODEL), lambda j: (0, 0)),
                  pl.BlockSpec((D_MODEL, D_MODEL), lambda j: (0, j)),
                  pl.BlockSpec((1, D_MODEL), lambda j: (0, j)),
                  vec, vec, vec, vec],
        out_specs=[pl.BlockSpec((bsz, D_MODEL), lambda j: (0, j)),
                   pl.BlockSpec((1, 128), lambda j: (0, 0))],
        out_shape=[jax.ShapeDtypeStruct((bsz, n_chunks * D_MODEL), F32),
                   jax.ShapeDtypeStruct((1, 128), F32)],
        name="ada",
    )(c, w_ada, b_ada.reshape(1, -1), lq1.reshape(1, -1), lk1.reshape(1, -1),
      lq2.reshape(1, -1), lk2.reshape(1, -1))


def _inproj_kernel(x_ref, mod_ref, g_ref, w_ref, wt_ref, o_ref, ot_ref, h_scr, ht_scr, *, n_nat):
    j = pl.program_id(2)

    @pl.when(j == 0)
    def _():
        h = _rms(x_ref[0], g_ref[...]) * (1.0 + mod_ref[0, 1:2, :]) + mod_ref[0, 0:1, :]
        h_scr[...] = h.astype(BF16)
        ht_scr[...] = h.T.astype(BF16)

    @pl.when(j < n_nat)
    def _():
        o_ref[0] = jnp.dot(h_scr[...], w_ref[...], preferred_element_type=F32).astype(BF16)

    @pl.when(j == n_nat)
    def _():
        ot_ref[0] = jnp.dot(wt_ref[...], ht_scr[...], preferred_element_type=F32).astype(BF16)


def _inproj(x, mod3, norm1_g, w_nat, w_tr):
    bsz, seq, _ = x.shape
    tm = min(TM_IN, seq)
    n_nat = D_NAT // TN_IN
    nat_j = lambda j: jnp.minimum(j, n_nat - 1)
    return pl.pallas_call(
        functools.partial(_inproj_kernel, n_nat=n_nat),
        grid=(bsz, seq // tm, n_nat + 1),
        in_specs=[pl.BlockSpec((1, tm, D_MODEL), lambda b, i, j: (b, i, 0)),
                  pl.BlockSpec((1, 6, D_MODEL), lambda b, i, j: (b, 0, 0)),
                  pl.BlockSpec((1, D_MODEL), lambda b, i, j: (0, 0)),
                  pl.BlockSpec((D_MODEL, TN_IN), lambda b, i, j: (0, nat_j(j))),
                  pl.BlockSpec((D_TR, D_MODEL), lambda b, i, j: (0, 0))],
        out_specs=[pl.BlockSpec((1, tm, TN_IN), lambda b, i, j: (b, i, nat_j(j))),
                   pl.BlockSpec((1, D_TR, tm), lambda b, i, j: (b, 0, i))],
        out_shape=[jax.ShapeDtypeStruct((bsz, seq, D_NAT), BF16),
                   jax.ShapeDtypeStruct((bsz, D_TR, seq), BF16)],
        scratch_shapes=[pltpu.VMEM((tm, D_MODEL), BF16), pltpu.VMEM((D_MODEL, tm), BF16)],
        compiler_params=pltpu.CompilerParams(
            dimension_semantics=("arbitrary", "arbitrary", "arbitrary"),
            vmem_limit_bytes=VMEM_LIMIT),
        name="inproj",
    )(x, mod3, norm1_g.reshape(1, -1), w_nat, w_tr)


def _attn_kernel(lam_ref, qt_ref, k_ref, vt_ref, bias_ref, g_ref, o_ref,
                 m_scr, acc_scr, vte_scr, s_a, s_b, t_a, t_b, *, tq):
    i = pl.program_id(2)
    heads = range(HEADS_PER_STEP)
    n_chain = 2 * HEADS_PER_STEP
    head_rows = lambda hh: slice(hh * V_DIM, (hh + 1) * V_DIM)
    sub_cols = lambda sub: slice(sub * tq, (sub + 1) * tq)

    @pl.when(i == 0)
    def _():
        for hh in heads:
            vte_scr[hh, 0:V_DIM, :] = vt_ref[0, head_rows(hh), :]
            vte_scr[hh, V_DIM:, :] = jnp.ones((ONES_ROWS, vte_scr.shape[2]), BF16)

    qt_maps = []
    for sub in range(Q_TILES_PER_STEP):
        maps = []
        for hh in heads:
            qt = qt_ref[0, head_rows(hh), sub_cols(sub)] * jnp.asarray(HEAD_DIM ** -0.5, BF16)
            row = lax.broadcasted_iota(jnp.int32, qt.shape, 0)
            zero = jnp.zeros_like(qt)
            maps += [jnp.where(row < HEAD_DIM, qt, zero), jnp.where(row >= HEAD_DIM, qt, zero)]
        qt_maps.append(maps)

    m_scr[...] = jnp.full(m_scr.shape, MASK_NEG, F32)
    acc_scr[...] = jnp.zeros(acc_scr.shape, F32)
    slots = ((s_a, t_a), (s_b, t_b))

    def produce(sub, j, bias, slot):
        s_ref, t_ref = slots[slot]
        for hh in heads:
            k = k_ref[0, pl.ds(pl.multiple_of(j * tq, tq), tq), head_rows(hh)]
            for m in range(2):
                c = 2 * hh + m
                s = jnp.dot(k, qt_maps[sub][c], preferred_element_type=F32)
                if bias is not None:
                    s = s + bias_ref[hh, bias]
                s_ref[c] = s
                t_ref[c] = jnp.max(s, axis=0, keepdims=True)

    def consume(sub, j, slot):
        s_ref, t_ref = slots[slot]
        for hh in heads:
            vte = vte_scr[hh, :, pl.ds(pl.multiple_of(j * tq, tq), tq)]
            for m in range(2):
                c = 2 * hh + m
                a = sub * n_chain + c
                m_prev = m_scr[a]
                m_new = jnp.maximum(m_prev, t_ref[c])
                alpha = jnp.exp(m_prev - m_new)
                p = jnp.exp(s_ref[c] - m_new).astype(BF16)
                acc_scr[a] = alpha * acc_scr[a] + jnp.dot(vte, p, preferred_element_type=F32)
                m_scr[a] = m_new

    diag, off1 = 0, 1

    def kind(behind):
        return diag if behind == 0 else off1 if behind == 1 else None

    group = Q_TILES_PER_STEP

    @pl.when(i == 0)
    def _():
        events = [(sub, t) for sub in range(group) for t in range(sub + 1)]
        produce(0, 0, kind(0), 0)
        for p, (sub, t) in enumerate(events):
            if p + 1 < len(events):
                nsub, nt = events[p + 1]
                produce(nsub, nt, kind(nsub - nt), (p + 1) % 2)
            consume(sub, t, p % 2)

    @pl.when(i >= 1)
    def _():
        produce(0, 0, None, 0)
        start = 0
        for sub in range(group):
            last = group * i + sub
            s0 = start

            def pair(r, carry, sub=sub, s0=s0):
                t = 2 * r
                produce(sub, t + 1, None, 1 - s0)
                consume(sub, t, s0)
                produce(sub, t + 2, None, s0)
                consume(sub, t + 1, 1 - s0)
                return carry

            n_pairs = (last - 2) // 2
            lax.fori_loop(0, n_pairs, pair, 0)
            rest = 3 if sub % 2 == 0 else 4
            for idx in range(rest):
                t = 2 * n_pairs + idx
                if idx + 1 < rest:
                    produce(sub, t + 1, kind(rest - 2 - idx), (s0 + idx + 1) % 2)
                elif sub + 1 < group:
                    produce(sub + 1, 0, None, (s0 + idx + 1) % 2)
                consume(sub, t, (s0 + idx) % 2)
            start = (s0 + rest) % 2

    lam = lam_ref[0]
    for sub in range(group):
        for hh in heads:
            a1, a2 = acc_scr[sub * n_chain + 2 * hh], acc_scr[sub * n_chain + 2 * hh + 1]
            ot = a1[:V_DIM] / a1[V_DIM:V_DIM + 1] - lam * (a2[:V_DIM] / a2[V_DIM:V_DIM + 1])
            ot = ot * lax.rsqrt(jnp.mean(ot * ot, axis=0, keepdims=True) + EPS) * g_ref[...]
            o_ref[0, sub_cols(sub), head_rows(hh)] = (ot * (1.0 - LAM_INIT)).T.astype(BF16)


def _attention(proj, proj_t, lam, bias_tiles, subln_g):
    bsz, seq, _ = proj.shape
    group = Q_TILES_PER_STEP
    tq = min(TQ, seq // group)
    n_groups = seq // (group * tq)
    hps = HEADS_PER_STEP
    width = hps * V_DIM
    kern = functools.partial(_attn_kernel, tq=tq)
    score = pltpu.VMEM((2 * hps, tq, tq), F32)
    colmax = pltpu.VMEM((2 * hps, 1, tq), F32)
    return pl.pallas_call(
        kern,
        grid=(bsz, N_HEADS // hps, n_groups),
        in_specs=[pl.BlockSpec(memory_space=pltpu.SMEM),
                  pl.BlockSpec((1, width, group * tq), lambda b, h, i: (b, ROW_QT // width + h, i)),
                  pl.BlockSpec((1, seq, width), lambda b, h, i: (b, 0, COL_K // width + h)),
                  pl.BlockSpec((1, width, seq), lambda b, h, i: (b, ROW_VT // width + h, 0)),
                  pl.BlockSpec((hps, 2, tq, tq), lambda b, h, i: (h, 0, 0, 0)),
                  pl.BlockSpec((V_DIM, 1), lambda b, h, i: (0, 0))],
        out_specs=pl.BlockSpec((1, group * tq, width), lambda b, h, i: (b, i, h)),
        out_shape=jax.ShapeDtypeStruct((bsz, seq, N_HEADS * V_DIM), BF16),
        scratch_shapes=[pltpu.VMEM((2 * group * hps, 1, tq), F32),
                        pltpu.VMEM((2 * group * hps, V_DIM + ONES_ROWS, tq), F32),
                        pltpu.VMEM((hps, V_DIM + ONES_ROWS, seq), BF16),
                        score, score, colmax, colmax],
        compiler_params=pltpu.CompilerParams(
            dimension_semantics=("arbitrary", "arbitrary", "arbitrary"),
            vmem_limit_bytes=VMEM_LIMIT),
        name="attn",
    )(lam, proj_t, proj, proj_t, bias_tiles, subln_g.reshape(-1, 1))


def _t5_bucket(dist):
    max_exact = N_BUCKETS // 2
    d = jnp.maximum(dist, 1).astype(F32)
    large = max_exact + (jnp.log(d / max_exact) / math.log(MAX_DISTANCE / max_exact)
                         * (N_BUCKETS - max_exact)).astype(jnp.int32)
    large = jnp.minimum(large, N_BUCKETS - 1)
    return jnp.where(dist < max_exact, dist, large)


def _bias_tiles(rel_table, tq):
    assert tq >= MAX_DISTANCE
    r = jnp.arange(tq, dtype=jnp.int32)
    rel0 = r[None, :] - r[:, None]
    rel = jnp.stack([rel0, rel0 + tq])
    table = (rel_table - rel_table[N_BUCKETS - 1:N_BUCKETS, :]).astype(F32)
    bucket = _t5_bucket(jnp.maximum(rel, 0))
    onehot = (bucket[None] == jnp.arange(N_BUCKETS, dtype=jnp.int32)[:, None, None, None]).astype(F32)
    bias = jnp.einsum('kh,kabc->habc', table, onehot, precision=lax.Precision.HIGHEST)
    return jnp.where((rel >= 0)[None], bias, MASK_NEG)


def _post_kernel(att_ref, ga_ref, gb_ref, cb_ref, cc_ref, cu_ref, x_ref, mod_ref,
                 cw_ref, cbias_ref, woc_ref, woa_ref, wmg_ref, g2_ref, wr_ref, br_ref,
                 x1_ref, hp_ref, idx_ref, tw_ref, cnt_ref,
                 ubuf, carry, *, tm, tiles_per_seq):
    i = pl.program_id(0)

    @pl.when(i % tiles_per_seq == 0)
    def _():
        carry[...] = jnp.zeros(carry.shape, F32)

    @pl.when(i == 0)
    def _():
        cnt_ref[...] = jnp.zeros(cnt_ref.shape, F32)

    u = cc_ref[...].astype(F32) * cu_ref[...].astype(F32)
    ubuf[0:8, :] = carry[...]
    ubuf[8:8 + tm, :] = u
    carry[...] = u[tm - 8:tm, :]
    conv = (cbias_ref[...] + ubuf[6:6 + tm, :] * cw_ref[0:1, :]
            + ubuf[7:7 + tm, :] * cw_ref[1:2, :] + u * cw_ref[2:3, :])
    ya = jnp.dot((cb_ref[...].astype(F32) * conv).astype(BF16), woc_ref[...],
                 preferred_element_type=F32)
    yb = jnp.dot(att_ref[...], woa_ref[...], preferred_element_type=F32)
    merged = _sigmoid(ga_ref[...].astype(F32)) * ya + _sigmoid(gb_ref[...].astype(F32)) * yb
    mix = jnp.dot(merged.astype(BF16), wmg_ref[...], preferred_element_type=F32)
    x1 = x_ref[...] + mod_ref[0, 2:3, :] * mix
    x1_ref[...] = x1

    h2 = _rms(x1, g2_ref[...]) * (1.0 + mod_ref[0, 4:5, :]) + mod_ref[0, 3:4, :]
    h_hi = h2.astype(BF16)
    h_hi32 = h_hi.astype(F32)
    h_lo = (h2 - h_hi32).astype(BF16)

    part = jnp.dot(h_hi, wr_ref[...], preferred_element_type=F32)
    logits = (part[:, :N_EXPERTS] + part[:, N_EXPERTS:]
              + jnp.dot(h_lo, wr_ref[:, :N_EXPERTS], preferred_element_type=F32)
              + br_ref[...])

    lane = lax.broadcasted_iota(jnp.int32, logits.shape, 1)
    work = logits
    sel_l, sel_i = [], []
    member = jnp.zeros(logits.shape, F32)
    for _ in range(TOP_K):
        mx = jnp.max(work, axis=-1, keepdims=True)
        ix = jnp.min(jnp.where(work == mx, lane, N_EXPERTS), axis=-1, keepdims=True)
        hit = lane == ix
        member = member + hit.astype(F32)
        work = jnp.where(hit, -jnp.inf, work)
        sel_l.append(mx)
        sel_i.append(ix)
    ex = [jnp.exp(l - sel_l[0]) for l in sel_l]
    den = ex[0] + ex[1] + ex[2] + ex[3]
    tw_ref[...] = jnp.concatenate([e / den for e in ex], axis=1)
    idx_ref[...] = jnp.concatenate(sel_i, axis=1)
    cnt_ref[...] += jnp.sum(member, axis=0, keepdims=True)

    hp_ref[...] = _pack_halves(h2)


def _post(att, proj, x2, mod3, conv_w, conv_b, woc_b, woa_b, wmg_b, norm2_g, wr_split, b_router, seq):
    n = x2.shape[0]
    tm = min(TM_POST, seq)
    tiles_per_seq = seq // tm
    kern = functools.partial(_post_kernel, tm=tm, tiles_per_seq=tiles_per_seq)
    row = lambda c: (lambda i: (i, c))
    const = lambda i: (0, 0)
    return pl.pallas_call(
        kern,
        grid=(n // tm,),
        in_specs=[pl.BlockSpec((tm, D_MODEL), row(0)),
                  pl.BlockSpec((tm, D_MODEL), row(COL_GA // D_MODEL)),
                  pl.BlockSpec((tm, D_MODEL), row(COL_GB // D_MODEL)),
                  pl.BlockSpec((tm, CONV_WIDTH), row(COL_CB // CONV_WIDTH)),
                  pl.BlockSpec((tm, CONV_WIDTH), row(COL_CC // CONV_WIDTH)),
                  pl.BlockSpec((tm, CONV_WIDTH), row(COL_CU // CONV_WIDTH)),
                  pl.BlockSpec((tm, D_MODEL), row(0)),
                  pl.BlockSpec((1, 6, D_MODEL), lambda i: (i // tiles_per_seq, 0, 0)),
                  pl.BlockSpec((CONV_K, CONV_WIDTH), const),
                  pl.BlockSpec((1, CONV_WIDTH), const),
                  pl.BlockSpec((CONV_WIDTH, D_MODEL), const),
                  pl.BlockSpec((D_MODEL, D_MODEL), const),
                  pl.BlockSpec((D_MODEL, D_MODEL), const),
                  pl.BlockSpec((1, D_MODEL), const),
                  pl.BlockSpec((D_MODEL, 2 * N_EXPERTS), const),
                  pl.BlockSpec((1, N_EXPERTS), const)],
        out_specs=[pl.BlockSpec((tm, D_MODEL), row(0)),
                   pl.BlockSpec((tm, D_MODEL // 2), row(0)),
                   pl.BlockSpec((tm, TOP_K), row(0)),
                   pl.BlockSpec((tm, TOP_K), row(0)),
                   pl.BlockSpec((1, N_EXPERTS), const)],
        out_shape=[jax.ShapeDtypeStruct((n, D_MODEL), F32),
                   jax.ShapeDtypeStruct((n, D_MODEL // 2), jnp.uint32),
                   jax.ShapeDtypeStruct((n, TOP_K), jnp.int32),
                   jax.ShapeDtypeStruct((n, TOP_K), F32),
                   jax.ShapeDtypeStruct((1, N_EXPERTS), F32)],
        scratch_shapes=[pltpu.VMEM((tm + 8, CONV_WIDTH), F32), pltpu.VMEM((8, CONV_WIDTH), F32)],
        compiler_params=pltpu.CompilerParams(dimension_semantics=("arbitrary",),
                                             vmem_limit_bytes=VMEM_LIMIT),
        name="post",
    )(att, proj, proj, proj, proj, proj, x2, mod3, conv_w, conv_b.reshape(1, -1),
      woc_b, woa_b, wmg_b, norm2_g.reshape(1, -1), wr_split, b_router.reshape(1, -1))


def _scan_kernel(idx_ref, pstart_ref, dest_ref, carry, earlier, *, t):
    @pl.when(pl.program_id(0) == 0)
    def _():
        carry[...] = jnp.zeros(carry.shape, F32)
        r = lax.broadcasted_iota(jnp.int32, (t, t), 0)
        c = lax.broadcasted_iota(jnp.int32, (t, t), 1)
        earlier[...] = jnp.where(c < r, 1.0, 0.0).astype(BF16)

    idx = idx_ref[...]
    lane = lax.broadcasted_iota(jnp.int32, (t, N_EXPERTS), 1)
    hits = [lane == idx[:, k:k + 1] for k in range(TOP_K)]
    member = hits[0].astype(F32)
    for k in range(1, TOP_K):
        member = member + hits[k].astype(F32)
    rank = jnp.dot(earlier[...], member.astype(BF16), preferred_element_type=F32) + carry[...]
    dest_e = pstart_ref[...] + rank
    cols = [jnp.sum(jnp.where(hits[k], dest_e, 0.0), axis=-1, keepdims=True) for k in range(TOP_K)]
    dest_ref[...] = jnp.concatenate(cols, axis=1).astype(jnp.int32)
    carry[...] += jnp.sum(member, axis=0, keepdims=True)


def _scan(idx, pstart):
    n = idx.shape[0]
    t = min(T_SCAN, n)
    return pl.pallas_call(
        functools.partial(_scan_kernel, t=t),
        grid=(n // t,),
        in_specs=[pl.BlockSpec((t, TOP_K), lambda i: (i, 0)),
                  pl.BlockSpec((1, N_EXPERTS), lambda i: (0, 0))],
        out_specs=pl.BlockSpec((t, TOP_K), lambda i: (i, 0)),
        out_shape=jax.ShapeDtypeStruct((n, TOP_K), jnp.int32),
        scratch_shapes=[pltpu.VMEM((1, N_EXPERTS), F32), pltpu.VMEM((t, t), BF16)],
        compiler_params=pltpu.CompilerParams(dimension_semantics=("arbitrary",)),
        name="scan",
    )(idx, pstart)


def _sc_mesh():
    return plsc.VectorSubcoreMesh(core_axis_name="core", subcore_axis_name="subcore")


def _sc_worker(mesh):
    return lax.axis_index("core") * mesh.num_subcores + lax.axis_index("subcore")


def _sc_dispatch(dest_t, hp, rows):
    n, width = hp.shape
    mesh = _sc_mesh()
    workers = mesh.num_cores * mesh.num_subcores
    per_worker = n // (SC_WINDOW * workers)
    assert per_worker * SC_WINDOW * workers == n

    @pl.kernel(out_type=jax.ShapeDtypeStruct((rows, width), hp.dtype), mesh=mesh,
               scratch_types=[pltpu.VMEM((SC_WINDOW, width), hp.dtype),
                              pltpu.VMEM((TOP_K, SC_WINDOW), jnp.int32)])
    def scatter(hp_hbm, dest_hbm, xs_hbm, row_buf, idx_buf):
        first = _sc_worker(mesh) * per_worker

        @pl.loop(0, per_worker)
        def _(w):
            t0 = (first + w) * SC_WINDOW
            pltpu.sync_copy(hp_hbm.at[pl.ds(t0, SC_WINDOW)], row_buf)
            pltpu.sync_copy(dest_hbm.at[:, pl.ds(t0, SC_WINDOW)], idx_buf)
            for k in range(TOP_K):
                pltpu.sync_copy(row_buf, xs_hbm.at[idx_buf.at[k]])

    return scatter(hp, dest_t)


def _sc_gather(src_rows, table):
    n_out = src_rows.shape[1]
    width = table.shape[1]
    mesh = _sc_mesh()
    workers = mesh.num_cores * mesh.num_subcores
    per_worker = n_out // (SC_WINDOW * workers)
    assert per_worker * SC_WINDOW * workers == n_out

    @pl.kernel(out_type=jax.ShapeDtypeStruct((n_out, width), table.dtype), mesh=mesh,
               scratch_types=[pltpu.VMEM((SC_WINDOW, width), table.dtype),
                              pltpu.VMEM((1, SC_WINDOW), jnp.int32)])
    def gather(table_hbm, src_hbm, out_hbm, row_buf, idx_buf):
        first = _sc_worker(mesh) * per_worker

        @pl.loop(0, per_worker)
        def _(w):
            r0 = (first + w) * SC_WINDOW
            pltpu.sync_copy(src_hbm.at[:, pl.ds(r0, SC_WINDOW)], idx_buf)
            pltpu.sync_copy(table_hbm.at[idx_buf.at[0]], row_buf)
            pltpu.sync_copy(row_buf, out_hbm.at[pl.ds(r0, SC_WINDOW)])

    return gather(table, src_rows)


def _expert_kernel(be_ref, first_ref, used_ref, valid_ref, xs_ref, wgu_ref, bg_ref, bl_ref, wd_ref,
                   bd_ref, y_ref, wg_s, wl_s, wd_s, tmp_s):
    del be_ref
    i = pl.program_id(0)

    @pl.when(first_ref[i] == 1)
    def _():
        half = DEINT_CHUNK // 2
        for c in range(wgu_ref.shape[2] // DEINT_CHUNK):
            for db in range(D_MODEL // LANES):
                rows = slice(db * LANES, (db + 1) * LANES)
                tmp_s[...] = wgu_ref[0, rows, c * DEINT_CHUNK:(c + 1) * DEINT_CHUNK].T
                feats = slice(c * half, (c + 1) * half)
                wg_s[feats, rows] = tmp_s[pl.ds(0, half, stride=2), :].astype(BF16)
                wl_s[feats, rows] = tmp_s[pl.ds(1, half, stride=2), :].astype(BF16)
        wd_s[...] = wd_ref[0].astype(BF16)

    @pl.when(i < used_ref[0])
    def _():
        xp = xs_ref[...]
        rid = lax.broadcasted_iota(jnp.int32, xp.shape, 0)
        xp = jnp.where(rid < valid_ref[i], xp, jnp.uint32(0))
        xb = jnp.concatenate(_unpack_halves(xp), axis=1).astype(BF16)
        nt = (((1,), (1,)), ((), ()))
        g = lax.dot_general(xb, wg_s[...], nt, preferred_element_type=F32) + bg_ref[0]
        l = lax.dot_general(xb, wl_s[...], nt, preferred_element_type=F32) + bl_ref[0]
        glu = jnp.minimum(g, SWIGLU_LIMIT)
        lin = jnp.clip(l, -SWIGLU_LIMIT, SWIGLU_LIMIT)
        act = glu * _sigmoid(SWIGLU_ALPHA * glu) * (lin + 1.0)
        y = jnp.dot(act.astype(BF16), wd_s[...], preferred_element_type=F32) + bd_ref[0]
        y_ref[...] = _pack_halves(y)

    @pl.when(i >= used_ref[0])
    def _():
        y_ref[...] = jnp.zeros(y_ref.shape, y_ref.dtype)


def _experts(block_e, first, n_used, valid, xs, wgu, bg, bl, wd, bd):
    rows, width = xs.shape
    d_exp = wd.shape[1]
    n_blocks = rows // ROW_BLOCK
    wsel = lambda i, be, fi, nu, va: (be[i], 0, 0)
    rsel = lambda i, be, fi, nu, va: (jnp.minimum(i, nu[0] - 1), 0)
    grid_spec = pltpu.PrefetchScalarGridSpec(
        num_scalar_prefetch=4,
        grid=(n_blocks,),
        in_specs=[pl.BlockSpec((ROW_BLOCK, width), rsel),
                  pl.BlockSpec((1, D_MODEL, 2 * d_exp), wsel),
                  pl.BlockSpec((1, 1, d_exp), wsel),
                  pl.BlockSpec((1, 1, d_exp), wsel),
                  pl.BlockSpec((1, d_exp, D_MODEL), wsel),
                  pl.BlockSpec((1, 1, D_MODEL), wsel)],
        out_specs=pl.BlockSpec((ROW_BLOCK, width), lambda i, be, fi, nu, va: (i, 0)),
        scratch_shapes=[pltpu.VMEM((d_exp, D_MODEL), BF16), pltpu.VMEM((d_exp, D_MODEL), BF16),
                        pltpu.VMEM((d_exp, D_MODEL), BF16), pltpu.VMEM((DEINT_CHUNK, LANES), F32)],
    )
    return pl.pallas_call(
        _expert_kernel,
        grid_spec=grid_spec,
        out_shape=jax.ShapeDtypeStruct((rows, width), xs.dtype),
        compiler_params=pltpu.CompilerParams(dimension_semantics=("arbitrary",),
                                             vmem_limit_bytes=VMEM_LIMIT),
        name="experts",
    )(block_e, first, n_used, valid, xs, wgu, bg, bl, wd, bd)


def _combine_kernel(y0_ref, y1_ref, y2_ref, y3_ref, x1_ref, tw_ref, mod_ref, gf_ref, *rest):
    o_ref = rest[-1]
    tw = tw_ref[...]
    lo = hi = None
    for k, yk_ref in enumerate((y0_ref, y1_ref, y2_ref, y3_ref)):
        y_lo, y_hi = _unpack_halves(yk_ref[...])
        w = tw[:, k:k + 1]
        lo = w * y_lo if lo is None else lo + w * y_lo
        hi = w * y_hi if hi is None else hi + w * y_hi
    moe = jnp.concatenate([lo, hi], axis=1)
    xo = x1_ref[...] + mod_ref[0, 5:6, :] * moe
    o_ref[...] = _rms(xo, gf_ref[...])


def _combine(yk, x1, tw, mod3, normf_g, seq, tile0, prev):
    n = x1.shape[0]
    t = min(TM_POST, seq)
    tiles_per_seq = seq // t
    n_c = yk.shape[0] // TOP_K
    slot = lambda k: pl.BlockSpec((t, yk.shape[1]), lambda i: (k * (n_c // t) + i, 0))
    tok = lambda w: pl.BlockSpec((t, w), lambda i: (tile0 + i, 0))
    in_specs = [slot(0), slot(1), slot(2), slot(3), tok(D_MODEL), tok(TOP_K),
                pl.BlockSpec((1, 6, D_MODEL), lambda i: ((tile0 + i) // tiles_per_seq, 0, 0)),
                pl.BlockSpec((1, D_MODEL), lambda i: (0, 0))]
    args = [yk, yk, yk, yk, x1, tw, mod3, normf_g.reshape(1, -1)]
    aliases = {}
    if prev is not None:
        aliases = {len(args): 0}
        in_specs.append(pl.BlockSpec(memory_space=pl.ANY))
        args.append(prev)
    return pl.pallas_call(
        _combine_kernel,
        grid=(n_c // t,),
        in_specs=in_specs,
        out_specs=tok(D_MODEL),
        out_shape=jax.ShapeDtypeStruct((n, D_MODEL), F32),
        input_output_aliases=aliases,
        compiler_params=pltpu.CompilerParams(dimension_semantics=("arbitrary",)),
        name="combine",
    )(*args)


def _in_proj_weights(w_in):
    c0 = 3 * CONV_WIDTH
    q, k, v = (w_in[:, c0 + s * D_MODEL:c0 + (s + 1) * D_MODEL] for s in range(3))
    w_nat = jnp.concatenate([k, w_in[:, c0 + 3 * D_MODEL:], w_in[:, :c0]], axis=1).astype(BF16)
    w_tr = jnp.concatenate([q, v], axis=1).T.astype(BF16)
    return w_nat, w_tr


def kernel(x, c, w_ada, b_ada, norm1_g, norm2_g, w_in, conv_w, conv_b, w_out_conv, lambda_q1, lambda_k1, lambda_q2, lambda_k2, subln_g, w_o_attn, w_merge, w_router, b_router, w_gate_up, b_gate_up, w_down, b_down, rel_bias_table, normf_g):
    bsz, seq, _ = x.shape
    n = bsz * seq
    assert w_ada.shape[0] == 1, "single layer"
    assert seq % (Q_TILES_PER_STEP * min(TQ, seq // Q_TILES_PER_STEP)) == 0 and n % TM_POST == 0 and (n * TOP_K) % ROW_BLOCK == 0

    mod, lam = _ada(c, w_ada[0], b_ada[0], lambda_q1[0], lambda_k1[0], lambda_q2[0], lambda_k2[0])
    mod3 = mod.reshape(bsz, 6, D_MODEL)

    proj, proj_t = _inproj(x, mod3, norm1_g[0], *_in_proj_weights(w_in[0]))
    att = _attention(proj, proj_t, lam[0, :1], _bias_tiles(rel_bias_table, min(TQ, seq // Q_TILES_PER_STEP)), subln_g[0])

    wr = w_router[0]
    wr_hi = wr.astype(BF16)
    wr_lo = (wr - wr_hi.astype(F32)).astype(BF16)
    x1, hp, idx, tw, counts = _post(
        att.reshape(n, -1), proj.reshape(n, -1), x.reshape(n, -1), mod3, conv_w[0], conv_b[0],
        w_out_conv[0].astype(BF16), w_o_attn[0].astype(BF16), w_merge[0].astype(BF16),
        norm2_g[0], jnp.concatenate([wr_hi, wr_lo], axis=1), b_router[0], seq)

    cnt = counts[0].astype(jnp.int32)
    padded = (cnt + ROW_BLOCK - 1) // ROW_BLOCK * ROW_BLOCK
    pend = jnp.cumsum(padded)
    pstart = pend - padded
    n_blocks = n * TOP_K // ROW_BLOCK + N_EXPERTS
    rows = n_blocks * ROW_BLOCK
    block_start = jnp.arange(n_blocks, dtype=jnp.int32) * ROW_BLOCK
    block_e = jnp.minimum(jnp.sum(pend[None, :] <= block_start[:, None], axis=1),
                          N_EXPERTS - 1).astype(jnp.int32)

    dest_t = _scan(idx, pstart.astype(F32).reshape(1, -1)).T
    xs = _sc_dispatch(dest_t, hp, rows)

    n_used = (pend[-1:] // ROW_BLOCK).astype(jnp.int32)
    changed = jnp.concatenate([jnp.ones((1,), jnp.bool_), block_e[1:] != block_e[:-1]])
    first = (changed & (jnp.arange(n_blocks) < n_used[0])).astype(jnp.int32)
    valid = jnp.clip((pstart + cnt)[block_e] - block_start, 0, ROW_BLOCK).astype(jnp.int32)
    bgu = b_gate_up[0]
    y = _experts(block_e, first, n_used, valid, xs, w_gate_up[0], bgu[:, None, 0::2],
                 bgu[:, None, 1::2], w_down[0], b_down[0][:, None, :])

    n_c = n // COMBINE_CHUNKS
    out = None
    for ch in range(COMBINE_CHUNKS):
        src = dest_t[:, ch * n_c:(ch + 1) * n_c].reshape(1, -1)
        out = _combine(_sc_gather(src, y), x1, tw, mod3, normf_g, seq, ch * n_c // min(TM_POST, seq), out)
    return out.reshape(bsz, seq, D_MODEL)
```

```python
import functools
import math

import jax
import jax.numpy as jnp
from jax import lax
from jax.experimental import pallas as pl
from jax.experimental.pallas import tpu as pltpu
from jax.experimental.pallas import tpu_sc as plsc

F32 = jnp.float32
BF16 = jnp.bfloat16

D_MODEL = 1024
CONV_WIDTH = 512
CONV_K = 3
N_HEADS = 8
HEAD_DIM = 64
V_DIM = 2 * HEAD_DIM
N_BUCKETS = 32
MAX_DISTANCE = 128
N_EXPERTS = 32
TOP_K = 4
SWIGLU_ALPHA = 1.702
SWIGLU_LIMIT = 7.0
EPS = 1e-6
LAM_INIT = 0.8 - 0.6 * math.exp(-0.3 * 0)

D_NAT = 3 * CONV_WIDTH + 3 * D_MODEL
COL_K, COL_GA, COL_GB = 0, 1024, 2048
COL_CB, COL_CC, COL_CU = 3072, 3584, 4096
D_TR = 2 * D_MODEL
ROW_QT, ROW_VT = 0, 1024

TM_IN = 1024
TN_IN = 1536
TQ = 512
HEADS_PER_STEP = 2
TM_POST = 512
T_SCAN = 512
SC_WINDOW = 128
COMBINE_CHUNKS = 2
ROW_BLOCK = 512
MASK_NEG = -1e30
ONES_ROWS = 16
LANES = 128
DEINT_CHUNK = 512
VMEM_LIMIT = 56 * 1024 * 1024


def _sigmoid(x):
    return 1.0 / (1.0 + jnp.exp(-x))


def _rms(x, g):
    return x * lax.rsqrt(jnp.mean(x * x, axis=-1, keepdims=True) + EPS) * g


def _pack_halves(v):
    bits = pltpu.bitcast(v.astype(BF16).astype(F32), jnp.uint32)
    w = v.shape[1] // 2
    return (bits[:, w:] & jnp.uint32(0xFFFF0000)) | (bits[:, :w] >> 16)


def _unpack_halves(words):
    return (pltpu.bitcast(words << 16, F32), pltpu.bitcast(words & jnp.uint32(0xFFFF0000), F32))


def _ada_kernel(c_ref, w_ref, b_ref, lq1_ref, lk1_ref, lq2_ref, lk2_ref, mod_ref, lam_ref):
    c = c_ref[...]
    s = c * _sigmoid(c)
    mod_ref[...] = jnp.dot(s, w_ref[...], precision=lax.Precision.HIGHEST,
                           preferred_element_type=F32) + b_ref[...]
    a1 = jnp.sum(lq1_ref[...] * lk1_ref[...], axis=-1, keepdims=True)
    a2 = jnp.sum(lq2_ref[...] * lk2_ref[...], axis=-1, keepdims=True)
    lam_ref[...] = jnp.broadcast_to(jnp.exp(a1) - jnp.exp(a2) + LAM_INIT, lam_ref.shape)


def _ada(c, w_ada, b_ada, lq1, lk1, lq2, lk2):
    bsz = c.shape[0]
    n_chunks = w_ada.shape[1] // D_MODEL
    vec = pl.BlockSpec((1, HEAD_DIM), lambda j: (0, 0))
    return pl.pallas_call(
        _ada_kernel,
        grid=(n_chunks,),
        in_specs=[pl.BlockSpec((bsz, D_MODEL), lambda j: (0, 0)),
                  pl.BlockSpec((D_MODEL, D_MODEL), lambda j: (0, j)),
                  pl.BlockSpec((1, D_MODEL), lambda j: (0, j)),
                  vec, vec, vec, vec],
        out_specs=[pl.BlockSpec((bsz, D_MODEL), lambda j: (0, j)),
                   pl.BlockSpec((1, 128), lambda j: (0, 0))],
        out_shape=[jax.ShapeDtypeStruct((bsz, n_chunks * D_MODEL), F32),
                   jax.ShapeDtypeStruct((1, 128), F32)],
        name="ada",
    )(c, w_ada, b_ada.reshape(1, -1), lq1.reshape(1, -1), lk1.reshape(1, -1),
      lq2.reshape(1, -1), lk2.reshape(1, -1))


def _inproj_kernel(x_ref, mod_ref, g_ref, w_ref, wt_ref, o_ref, ot_ref, h_scr, ht_scr, *, n_nat):
    j = pl.program_id(2)

    @pl.when(j == 0)
    def _():
        h = _rms(x_ref[0], g_ref[...]) * (1.0 + mod_ref[0, 1:2, :]) + mod_ref[0, 0:1, :]
        h_scr[...] = h.astype(BF16)
        ht_scr[...] = h.T.astype(BF16)

    @pl.when(j < n_nat)
    def _():
        o_ref[0] = jnp.dot(h_scr[...], w_ref[...], preferred_element_type=F32).astype(BF16)

    @pl.when(j == n_nat)
    def _():
        ot_ref[0] = jnp.dot(wt_ref[...], ht_scr[...], preferred_element_type=F32).astype(BF16)


def _inproj(x, mod3, norm1_g, w_nat, w_tr):
    bsz, seq, _ = x.shape
    tm = min(TM_IN, seq)
    n_nat = D_NAT // TN_IN
    nat_j = lambda j: jnp.minimum(j, n_nat - 1)
    return pl.pallas_call(
        functools.partial(_inproj_kernel, n_nat=n_nat),
        grid=(bsz, seq // tm, n_nat + 1),
        in_specs=[pl.BlockSpec((1, tm, D_MODEL), lambda b, i, j: (b, i, 0)),
                  pl.BlockSpec((1, 6, D_MODEL), lambda b, i, j: (b, 0, 0)),
                  pl.BlockSpec((1, D_MODEL), lambda b, i, j: (0, 0)),
                  pl.BlockSpec((D_MODEL, TN_IN), lambda b, i, j: (0, nat_j(j))),
                  pl.BlockSpec((D_TR, D_MODEL), lambda b, i, j: (0, 0))],
        out_specs=[pl.BlockSpec((1, tm, TN_IN), lambda b, i, j: (b, i, nat_j(j))),
                   pl.BlockSpec((1, D_TR, tm), lambda b, i, j: (b, 0, i))],
        out_shape=[jax.ShapeDtypeStruct((bsz, seq, D_NAT), BF16),
                   jax.ShapeDtypeStruct((bsz, D_TR, seq), BF16)],
        scratch_shapes=[pltpu.VMEM((tm, D_MODEL), BF16), pltpu.VMEM((D_MODEL, tm), BF16)],
        compiler_params=pltpu.CompilerParams(
            dimension_semantics=("arbitrary", "arbitrary", "arbitrary"),
            vmem_limit_bytes=VMEM_LIMIT),
        name="inproj",
    )(x, mod3, norm1_g.reshape(1, -1), w_nat, w_tr)


def _attn_kernel(lam_ref, qt_ref, k_ref, vt_ref, bias_ref, g_ref, o_ref,
                 m_scr, acc_scr, vte_scr, s_a, s_b, t_a, t_b, *, tq):
    i = pl.program_id(2)
    heads = range(HEADS_PER_STEP)
    n_chain = 2 * HEADS_PER_STEP
    head_rows = lambda hh: slice(hh * V_DIM, (hh + 1) * V_DIM)
    sub_cols = lambda sub: slice(sub * tq, (sub + 1) * tq)

    @pl.when(i == 0)
    def _():
        for hh in heads:
            vte_scr[hh, 0:V_DIM, :] = vt_ref[0, head_rows(hh), :]
            vte_scr[hh, V_DIM:, :] = jnp.ones((ONES_ROWS, vte_scr.shape[2]), BF16)

    qt_maps = []
    for sub in range(2):
        maps = []
        for hh in heads:
            qt = qt_ref[0, head_rows(hh), sub_cols(sub)] * jnp.asarray(HEAD_DIM ** -0.5, BF16)
            row = lax.broadcasted_iota(jnp.int32, qt.shape, 0)
            zero = jnp.zeros_like(qt)
            maps += [jnp.where(row < HEAD_DIM, qt, zero), jnp.where(row >= HEAD_DIM, qt, zero)]
        qt_maps.append(maps)

    m_scr[...] = jnp.full(m_scr.shape, MASK_NEG, F32)
    acc_scr[...] = jnp.zeros(acc_scr.shape, F32)
    slots = ((s_a, t_a), (s_b, t_b))

    def produce(sub, j, bias, slot):
        s_ref, t_ref = slots[slot]
        for hh in heads:
            k = k_ref[0, pl.ds(pl.multiple_of(j * tq, tq), tq), head_rows(hh)]
            for m in range(2):
                c = 2 * hh + m
                s = jnp.dot(k, qt_maps[sub][c], preferred_element_type=F32)
                if bias is not None:
                    s = s + bias_ref[hh, bias]
                s_ref[c] = s
                t_ref[c] = jnp.max(s, axis=0, keepdims=True)

    def consume(sub, j, slot):
        s_ref, t_ref = slots[slot]
        for hh in heads:
            vte = vte_scr[hh, :, pl.ds(pl.multiple_of(j * tq, tq), tq)]
            for m in range(2):
                c = 2 * hh + m
                a = sub * n_chain + c
                m_prev = m_scr[a]
                m_new = jnp.maximum(m_prev, t_ref[c])
                alpha = jnp.exp(m_prev - m_new)
                p = jnp.exp(s_ref[c] - m_new).astype(BF16)
                acc_scr[a] = alpha * acc_scr[a] + jnp.dot(vte, p, preferred_element_type=F32)
                m_scr[a] = m_new

    diag, off1 = 0, 1
    qa, qb = 0, 1

    @pl.when(i == 0)
    def _():
        produce(qa, 0, diag, 0)
        produce(qb, 0, off1, 1)
        consume(qa, 0, 0)
        produce(qb, 1, diag, 0)
        consume(qb, 0, 1)
        consume(qb, 1, 0)

    @pl.when(i >= 1)
    def _():
        ia = 2 * i
        ib = ia + 1
        produce(qa, 0, None, 0)

        def pair_a(r, carry):
            t = 2 * r
            produce(qa, t + 1, None, 1)
            consume(qa, t, 0)
            produce(qa, t + 2, None, 0)
            consume(qa, t + 1, 1)
            return carry

        lax.fori_loop(0, i - 1, pair_a, 0)
        produce(qa, ia - 1, off1, 1)
        consume(qa, ia - 2, 0)
        produce(qa, ia, diag, 0)
        consume(qa, ia - 1, 1)
        produce(qb, 0, None, 1)
        consume(qa, ia, 0)

        def pair_b(r, carry):
            t = 2 * r
            produce(qb, t + 1, None, 0)
            consume(qb, t, 1)
            produce(qb, t + 2, None, 1)
            consume(qb, t + 1, 0)
            return carry

        lax.fori_loop(0, i - 1, pair_b, 0)
        produce(qb, ib - 2, None, 0)
        consume(qb, ib - 3, 1)
        produce(qb, ib - 1, off1, 1)
        consume(qb, ib - 2, 0)
        produce(qb, ib, diag, 0)
        consume(qb, ib - 1, 1)
        consume(qb, ib, 0)

    lam = lam_ref[0]
    for sub in range(2):
        for hh in heads:
            a1, a2 = acc_scr[sub * n_chain + 2 * hh], acc_scr[sub * n_chain + 2 * hh + 1]
            ot = a1[:V_DIM] / a1[V_DIM:V_DIM + 1] - lam * (a2[:V_DIM] / a2[V_DIM:V_DIM + 1])
            ot = ot * lax.rsqrt(jnp.mean(ot * ot, axis=0, keepdims=True) + EPS) * g_ref[...]
            o_ref[0, sub_cols(sub), head_rows(hh)] = (ot * (1.0 - LAM_INIT)).T.astype(BF16)


def _attention(proj, proj_t, lam, bias_tiles, subln_g):
    bsz, seq, _ = proj.shape
    tq = min(TQ, seq // 2)
    n_pairs = seq // (2 * tq)
    hps = HEADS_PER_STEP
    width = hps * V_DIM
    kern = functools.partial(_attn_kernel, tq=tq)
    score = pltpu.VMEM((2 * hps, tq, tq), F32)
    colmax = pltpu.VMEM((2 * hps, 1, tq), F32)
    return pl.pallas_call(
        kern,
        grid=(bsz, N_HEADS // hps, n_pairs),
        in_specs=[pl.BlockSpec(memory_space=pltpu.SMEM),
                  pl.BlockSpec((1, width, 2 * tq), lambda b, h, i: (b, ROW_QT // width + h, i)),
                  pl.BlockSpec((1, seq, width), lambda b, h, i: (b, 0, COL_K // width + h)),
                  pl.BlockSpec((1, width, seq), lambda b, h, i: (b, ROW_VT // width + h, 0)),
                  pl.BlockSpec((hps, 2, tq, tq), lambda b, h, i: (h, 0, 0, 0)),
                  pl.BlockSpec((V_DIM, 1), lambda b, h, i: (0, 0))],
        out_specs=pl.BlockSpec((1, 2 * tq, width), lambda b, h, i: (b, i, h)),
        out_shape=jax.ShapeDtypeStruct((bsz, seq, N_HEADS * V_DIM), BF16),
        scratch_shapes=[pltpu.VMEM((4 * hps, 1, tq), F32),
                        pltpu.VMEM((4 * hps, V_DIM + ONES_ROWS, tq), F32),
                        pltpu.VMEM((hps, V_DIM + ONES_ROWS, seq), BF16),
                        score, score, colmax, colmax],
        compiler_params=pltpu.CompilerParams(
            dimension_semantics=("arbitrary", "arbitrary", "arbitrary"),
            vmem_limit_bytes=VMEM_LIMIT),
        name="attn",
    )(lam, proj_t, proj, proj_t, bias_tiles, subln_g.reshape(-1, 1))


def _t5_bucket(dist):
    max_exact = N_BUCKETS // 2
    d = jnp.maximum(dist, 1).astype(F32)
    large = max_exact + (jnp.log(d / max_exact) / math.log(MAX_DISTANCE / max_exact)
                         * (N_BUCKETS - max_exact)).astype(jnp.int32)
    large = jnp.minimum(large, N_BUCKETS - 1)
    return jnp.where(dist < max_exact, dist, large)


def _bias_tiles(rel_table, tq):
    assert tq >= MAX_DISTANCE
    r = jnp.arange(tq, dtype=jnp.int32)
    rel0 = r[None, :] - r[:, None]
    rel = jnp.stack([rel0, rel0 + tq])
    table = (rel_table - rel_table[N_BUCKETS - 1:N_BUCKETS, :]).astype(F32)
    bucket = _t5_bucket(jnp.maximum(rel, 0))
    onehot = (bucket[None] == jnp.arange(N_BUCKETS, dtype=jnp.int32)[:, None, None, None]).astype(F32)
    bias = jnp.einsum('kh,kabc->habc', table, onehot, precision=lax.Precision.HIGHEST)
    return jnp.where((rel >= 0)[None], bias, MASK_NEG)


def _post_kernel(att_ref, ga_ref, gb_ref, cb_ref, cc_ref, cu_ref, x_ref, mod_ref,
                 cw_ref, cbias_ref, woc_ref, woa_ref, wmg_ref, g2_ref, wr_ref, br_ref,
                 x1_ref, hp_ref, idx_ref, tw_ref, cnt_ref,
                 ubuf, carry, *, tm, tiles_per_seq):
    i = pl.program_id(0)

    @pl.when(i % tiles_per_seq == 0)
    def _():
        carry[...] = jnp.zeros(carry.shape, F32)

    @pl.when(i == 0)
    def _():
        cnt_ref[...] = jnp.zeros(cnt_ref.shape, F32)

    u = cc_ref[...].astype(F32) * cu_ref[...].astype(F32)
    ubuf[0:8, :] = carry[...]
    ubuf[8:8 + tm, :] = u
    carry[...] = u[tm - 8:tm, :]
    conv = (cbias_ref[...] + ubuf[6:6 + tm, :] * cw_ref[0:1, :]
            + ubuf[7:7 + tm, :] * cw_ref[1:2, :] + u * cw_ref[2:3, :])
    ya = jnp.dot((cb_ref[...].astype(F32) * conv).astype(BF16), woc_ref[...],
                 preferred_element_type=F32)
    yb = jnp.dot(att_ref[...], woa_ref[...], preferred_element_type=F32)
    merged = _sigmoid(ga_ref[...].astype(F32)) * ya + _sigmoid(gb_ref[...].astype(F32)) * yb
    mix = jnp.dot(merged.astype(BF16), wmg_ref[...], preferred_element_type=F32)
    x1 = x_ref[...] + mod_ref[0, 2:3, :] * mix
    x1_ref[...] = x1

    h2 = _rms(x1, g2_ref[...]) * (1.0 + mod_ref[0, 4:5, :]) + mod_ref[0, 3:4, :]
    h_hi = h2.astype(BF16)
    h_hi32 = h_hi.astype(F32)
    h_lo = (h2 - h_hi32).astype(BF16)

    part = jnp.dot(h_hi, wr_ref[...], preferred_element_type=F32)
    logits = (part[:, :N_EXPERTS] + part[:, N_EXPERTS:]
              + jnp.dot(h_lo, wr_ref[:, :N_EXPERTS], preferred_element_type=F32)
              + br_ref[...])

    lane = lax.broadcasted_iota(jnp.int32, logits.shape, 1)
    work = logits
    sel_l, sel_i = [], []
    member = jnp.zeros(logits.shape, F32)
    for _ in range(TOP_K):
        mx = jnp.max(work, axis=-1, keepdims=True)
        ix = jnp.min(jnp.where(work == mx, lane, N_EXPERTS), axis=-1, keepdims=True)
        hit = lane == ix
        member = member + hit.astype(F32)
        work = jnp.where(hit, -jnp.inf, work)
        sel_l.append(mx)
        sel_i.append(ix)
    ex = [jnp.exp(l - sel_l[0]) for l in sel_l]
    den = ex[0] + ex[1] + ex[2] + ex[3]
    tw_ref[...] = jnp.concatenate([e / den for e in ex], axis=1)
    idx_ref[...] = jnp.concatenate(sel_i, axis=1)
    cnt_ref[...] += jnp.sum(member, axis=0, keepdims=True)

    hp_ref[...] = _pack_halves(h2)


def _post(att, proj, x2, mod3, conv_w, conv_b, woc_b, woa_b, wmg_b, norm2_g, wr_split, b_router, seq):
    n = x2.shape[0]
    tm = min(TM_POST, seq)
    tiles_per_seq = seq // tm
    kern = functools.partial(_post_kernel, tm=tm, tiles_per_seq=tiles_per_seq)
    row = lambda c: (lambda i: (i, c))
    const = lambda i: (0, 0)
    return pl.pallas_call(
        kern,
        grid=(n // tm,),
        in_specs=[pl.BlockSpec((tm, D_MODEL), row(0)),
                  pl.BlockSpec((tm, D_MODEL), row(COL_GA // D_MODEL)),
                  pl.BlockSpec((tm, D_MODEL), row(COL_GB // D_MODEL)),
                  pl.BlockSpec((tm, CONV_WIDTH), row(COL_CB // CONV_WIDTH)),
                  pl.BlockSpec((tm, CONV_WIDTH), row(COL_CC // CONV_WIDTH)),
                  pl.BlockSpec((tm, CONV_WIDTH), row(COL_CU // CONV_WIDTH)),
                  pl.BlockSpec((tm, D_MODEL), row(0)),
                  pl.BlockSpec((1, 6, D_MODEL), lambda i: (i // tiles_per_seq, 0, 0)),
                  pl.BlockSpec((CONV_K, CONV_WIDTH), const),
                  pl.BlockSpec((1, CONV_WIDTH), const),
                  pl.BlockSpec((CONV_WIDTH, D_MODEL), const),
                  pl.BlockSpec((D_MODEL, D_MODEL), const),
                  pl.BlockSpec((D_MODEL, D_MODEL), const),
                  pl.BlockSpec((1, D_MODEL), const),
                  pl.BlockSpec((D_MODEL, 2 * N_EXPERTS), const),
                  pl.BlockSpec((1, N_EXPERTS), const)],
        out_specs=[pl.BlockSpec((tm, D_MODEL), row(0)),
                   pl.BlockSpec((tm, D_MODEL // 2), row(0)),
                   pl.BlockSpec((tm, TOP_K), row(0)),
                   pl.BlockSpec((tm, TOP_K), row(0)),
                   pl.BlockSpec((1, N_EXPERTS), const)],
        out_shape=[jax.ShapeDtypeStruct((n, D_MODEL), F32),
                   jax.ShapeDtypeStruct((n, D_MODEL // 2), jnp.uint32),
                   jax.ShapeDtypeStruct((n, TOP_K), jnp.int32),
                   jax.ShapeDtypeStruct((n, TOP_K), F32),
                   jax.ShapeDtypeStruct((1, N_EXPERTS), F32)],
        scratch_shapes=[pltpu.VMEM((tm + 8, CONV_WIDTH), F32), pltpu.VMEM((8, CONV_WIDTH), F32)],
        compiler_params=pltpu.CompilerParams(dimension_semantics=("arbitrary",),
                                             vmem_limit_bytes=VMEM_LIMIT),
        name="post",
    )(att, proj, proj, proj, proj, proj, x2, mod3, conv_w, conv_b.reshape(1, -1),
      woc_b, woa_b, wmg_b, norm2_g.reshape(1, -1), wr_split, b_router.reshape(1, -1))


def _scan_kernel(idx_ref, pstart_ref, dest_ref, carry, earlier, *, t):
    @pl.when(pl.program_id(0) == 0)
    def _():
        carry[...] = jnp.zeros(carry.shape, F32)
        r = lax.broadcasted_iota(jnp.int32, (t, t), 0)
        c = lax.broadcasted_iota(jnp.int32, (t, t), 1)
        earlier[...] = jnp.where(c < r, 1.0, 0.0).astype(BF16)

    idx = idx_ref[...]
    lane = lax.broadcasted_iota(jnp.int32, (t, N_EXPERTS), 1)
    hits = [lane == idx[:, k:k + 1] for k in range(TOP_K)]
    member = hits[0].astype(F32)
    for k in range(1, TOP_K):
        member = member + hits[k].astype(F32)
    rank = jnp.dot(earlier[...], member.astype(BF16), preferred_element_type=F32) + carry[...]
    dest_e = pstart_ref[...] + rank
    cols = [jnp.sum(jnp.where(hits[k], dest_e, 0.0), axis=-1, keepdims=True) for k in range(TOP_K)]
    dest_ref[...] = jnp.concatenate(cols, axis=1).astype(jnp.int32)
    carry[...] += jnp.sum(member, axis=0, keepdims=True)


def _scan(idx, pstart):
    n = idx.shape[0]
    t = min(T_SCAN, n)
    return pl.pallas_call(
        functools.partial(_scan_kernel, t=t),
        grid=(n // t,),
        in_specs=[pl.BlockSpec((t, TOP_K), lambda i: (i, 0)),
                  pl.BlockSpec((1, N_EXPERTS), lambda i: (0, 0))],
        out_specs=pl.BlockSpec((t, TOP_K), lambda i: (i, 0)),
        out_shape=jax.ShapeDtypeStruct((n, TOP_K), jnp.int32),
        scratch_shapes=[pltpu.VMEM((1, N_EXPERTS), F32), pltpu.VMEM((t, t), BF16)],
        compiler_params=pltpu.CompilerParams(dimension_semantics=("arbitrary",)),
        name="scan",
    )(idx, pstart)


def _sc_mesh():
    return plsc.VectorSubcoreMesh(core_axis_name="core", subcore_axis_name="subcore")


def _sc_worker(mesh):
    return lax.axis_index("core") * mesh.num_subcores + lax.axis_index("subcore")


def _sc_dispatch(dest_t, hp, rows):
    n, width = hp.shape
    mesh = _sc_mesh()
    workers = mesh.num_cores * mesh.num_subcores
    per_worker = n // (SC_WINDOW * workers)
    assert per_worker * SC_WINDOW * workers == n

    @pl.kernel(out_type=jax.ShapeDtypeStruct((rows, width), hp.dtype), mesh=mesh,
               scratch_types=[pltpu.VMEM((SC_WINDOW, width), hp.dtype),
                              pltpu.VMEM((TOP_K, SC_WINDOW), jnp.int32)]
               + [pltpu.SemaphoreType.DMA] * TOP_K)
    def scatter(hp_hbm, dest_hbm, xs_hbm, row_buf, idx_buf, *sems):
        first = _sc_worker(mesh) * per_worker

        @pl.loop(0, per_worker)
        def _(w):
            t0 = (first + w) * SC_WINDOW
            pltpu.sync_copy(hp_hbm.at[pl.ds(t0, SC_WINDOW)], row_buf)
            pltpu.sync_copy(dest_hbm.at[:, pl.ds(t0, SC_WINDOW)], idx_buf)
            copies = [pltpu.async_copy(row_buf, xs_hbm.at[idx_buf.at[k]], sems[k])
                      for k in range(TOP_K)]
            for copy in copies:
                copy.wait()

    return scatter(hp, dest_t)


def _sc_gather(src_rows, table):
    n_out = src_rows.shape[1]
    width = table.shape[1]
    mesh = _sc_mesh()
    workers = mesh.num_cores * mesh.num_subcores
    per_worker = n_out // (SC_WINDOW * workers)
    assert per_worker * SC_WINDOW * workers == n_out

    half = SC_WINDOW // 2

    @pl.kernel(out_type=jax.ShapeDtypeStruct((n_out, width), table.dtype), mesh=mesh,
               scratch_types=[pltpu.VMEM((half, width), table.dtype),
                              pltpu.VMEM((half, width), table.dtype),
                              pltpu.VMEM((1, SC_WINDOW), jnp.int32)]
               + [pltpu.SemaphoreType.DMA] * 4)
    def gather(table_hbm, src_hbm, out_hbm, buf_a, buf_b, idx_buf, in_a, in_b, out_a, out_b):
        first = _sc_worker(mesh) * per_worker

        @pl.loop(0, per_worker)
        def _(w):
            r0 = (first + w) * SC_WINDOW
            pltpu.sync_copy(src_hbm.at[:, pl.ds(r0, SC_WINDOW)], idx_buf)
            get_a = pltpu.async_copy(table_hbm.at[idx_buf.at[0, pl.ds(0, half)]], buf_a, in_a)
            get_b = pltpu.async_copy(table_hbm.at[idx_buf.at[0, pl.ds(half, half)]], buf_b, in_b)
            get_a.wait()
            put_a = pltpu.async_copy(buf_a, out_hbm.at[pl.ds(r0, half)], out_a)
            get_b.wait()
            put_b = pltpu.async_copy(buf_b, out_hbm.at[pl.ds(r0 + half, half)], out_b)
            put_a.wait()
            put_b.wait()

    return gather(table, src_rows)


def _expert_kernel(be_ref, first_ref, used_ref, valid_ref, xs_ref, wgu_ref, bg_ref, bl_ref, wd_ref,
                   bd_ref, y_ref, wg_s, wl_s, wd_s, tmp_s):
    del be_ref
    i = pl.program_id(0)

    @pl.when(first_ref[i] == 1)
    def _():
        half = DEINT_CHUNK // 2
        for c in range(wgu_ref.shape[2] // DEINT_CHUNK):
            for db in range(D_MODEL // LANES):
                rows = slice(db * LANES, (db + 1) * LANES)
                tmp_s[...] = wgu_ref[0, rows, c * DEINT_CHUNK:(c + 1) * DEINT_CHUNK].T
                feats = slice(c * half, (c + 1) * half)
                wg_s[feats, rows] = tmp_s[pl.ds(0, half, stride=2), :].astype(BF16)
                wl_s[feats, rows] = tmp_s[pl.ds(1, half, stride=2), :].astype(BF16)
        wd_s[...] = wd_ref[0].astype(BF16)

    @pl.when(i < used_ref[0])
    def _():
        xp = xs_ref[...]
        rid = lax.broadcasted_iota(jnp.int32, xp.shape, 0)
        xp = jnp.where(rid < valid_ref[i], xp, jnp.uint32(0))
        xb = jnp.concatenate(_unpack_halves(xp), axis=1).astype(BF16)
        nt = (((1,), (1,)), ((), ()))
        g = lax.dot_general(xb, wg_s[...], nt, preferred_element_type=F32) + bg_ref[0]
        l = lax.dot_general(xb, wl_s[...], nt, preferred_element_type=F32) + bl_ref[0]
        glu = jnp.minimum(g, SWIGLU_LIMIT)
        lin = jnp.clip(l, -SWIGLU_LIMIT, SWIGLU_LIMIT)
        act = glu * _sigmoid(SWIGLU_ALPHA * glu) * (lin + 1.0)
        y = jnp.dot(act.astype(BF16), wd_s[...], preferred_element_type=F32) + bd_ref[0]
        y_ref[...] = _pack_halves(y)

    @pl.when(i >= used_ref[0])
    def _():
        y_ref[...] = jnp.zeros(y_ref.shape, y_ref.dtype)


def _experts(block_e, first, n_used, valid, xs, wgu, bg, bl, wd, bd):
    rows, width = xs.shape
    d_exp = wd.shape[1]
    n_blocks = rows // ROW_BLOCK
    wsel = lambda i, be, fi, nu, va: (be[i], 0, 0)
    rsel = lambda i, be, fi, nu, va: (jnp.minimum(i, nu[0] - 1), 0)
    grid_spec = pltpu.PrefetchScalarGridSpec(
        num_scalar_prefetch=4,
        grid=(n_blocks,),
        in_specs=[pl.BlockSpec((ROW_BLOCK, width), rsel),
                  pl.BlockSpec((1, D_MODEL, 2 * d_exp), wsel),
                  pl.BlockSpec((1, 1, d_exp), wsel),
                  pl.BlockSpec((1, 1, d_exp), wsel),
                  pl.BlockSpec((1, d_exp, D_MODEL), wsel),
                  pl.BlockSpec((1, 1, D_MODEL), wsel)],
        out_specs=pl.BlockSpec((ROW_BLOCK, width), lambda i, be, fi, nu, va: (i, 0)),
        scratch_shapes=[pltpu.VMEM((d_exp, D_MODEL), BF16), pltpu.VMEM((d_exp, D_MODEL), BF16),
                        pltpu.VMEM((d_exp, D_MODEL), BF16), pltpu.VMEM((DEINT_CHUNK, LANES), F32)],
    )
    return pl.pallas_call(
        _expert_kernel,
        grid_spec=grid_spec,
        out_shape=jax.ShapeDtypeStruct((rows, width), xs.dtype),
        compiler_params=pltpu.CompilerParams(dimension_semantics=("arbitrary",),
                                             vmem_limit_bytes=VMEM_LIMIT),
        name="experts",
    )(block_e, first, n_used, valid, xs, wgu, bg, bl, wd, bd)


def _combine_kernel(y0_ref, y1_ref, y2_ref, y3_ref, x1_ref, tw_ref, mod_ref, gf_ref, *rest):
    o_ref = rest[-1]
    tw = tw_ref[...]
    lo = hi = None
    for k, yk_ref in enumerate((y0_ref, y1_ref, y2_ref, y3_ref)):
        y_lo, y_hi = _unpack_halves(yk_ref[...])
        w = tw[:, k:k + 1]
        lo = w * y_lo if lo is None else lo + w * y_lo
        hi = w * y_hi if hi is None else hi + w * y_hi
    moe = jnp.concatenate([lo, hi], axis=1)
    xo = x1_ref[...] + mod_ref[0, 5:6, :] * moe
    o_ref[...] = _rms(xo, gf_ref[...])


def _combine(yk, x1, tw, mod3, normf_g, seq, tile0, prev):
    n = x1.shape[0]
    t = min(TM_POST, seq)
    tiles_per_seq = seq // t
    n_c = yk.shape[0] // TOP_K
    slot = lambda k: pl.BlockSpec((t, yk.shape[1]), lambda i: (k * (n_c // t) + i, 0))
    tok = lambda w: pl.BlockSpec((t, w), lambda i: (tile0 + i, 0))
    in_specs = [slot(0), slot(1), slot(2), slot(3), tok(D_MODEL), tok(TOP_K),
                pl.BlockSpec((1, 6, D_MODEL), lambda i: ((tile0 + i) // tiles_per_seq, 0, 0)),
                pl.BlockSpec((1, D_MODEL), lambda i: (0, 0))]
    args = [yk, yk, yk, yk, x1, tw, mod3, normf_g.reshape(1, -1)]
    aliases = {}
    if prev is not None:
        aliases = {len(args): 0}
        in_specs.append(pl.BlockSpec(memory_space=pl.ANY))
        args.append(prev)
    return pl.pallas_call(
        _combine_kernel,
        grid=(n_c // t,),
        in_specs=in_specs,
        out_specs=tok(D_MODEL),
        out_shape=jax.ShapeDtypeStruct((n, D_MODEL), F32),
        input_output_aliases=aliases,
        compiler_params=pltpu.CompilerParams(dimension_semantics=("arbitrary",)),
        name="combine",
    )(*args)


def _in_proj_weights(w_in):
    c0 = 3 * CONV_WIDTH
    q, k, v = (w_in[:, c0 + s * D_MODEL:c0 + (s + 1) * D_MODEL] for s in range(3))
    w_nat = jnp.concatenate([k, w_in[:, c0 + 3 * D_MODEL:], w_in[:, :c0]], axis=1).astype(BF16)
    w_tr = jnp.concatenate([q, v], axis=1).T.astype(BF16)
    return w_nat, w_tr


def kernel(x, c, w_ada, b_ada, norm1_g, norm2_g, w_in, conv_w, conv_b, w_out_conv, lambda_q1, lambda_k1, lambda_q2, lambda_k2, subln_g, w_o_attn, w_merge, w_router, b_router, w_gate_up, b_gate_up, w_down, b_down, rel_bias_table, normf_g):
    bsz, seq, _ = x.shape
    n = bsz * seq
    assert w_ada.shape[0] == 1, "single layer"
    assert seq % (2 * min(TQ, seq // 2)) == 0 and n % TM_POST == 0 and (n * TOP_K) % ROW_BLOCK == 0

    mod, lam = _ada(c, w_ada[0], b_ada[0], lambda_q1[0], lambda_k1[0], lambda_q2[0], lambda_k2[0])
    mod3 = mod.reshape(bsz, 6, D_MODEL)

    proj, proj_t = _inproj(x, mod3, norm1_g[0], *_in_proj_weights(w_in[0]))
    att = _attention(proj, proj_t, lam[0, :1], _bias_tiles(rel_bias_table, min(TQ, seq // 2)), subln_g[0])

    wr = w_router[0]
    wr_hi = wr.astype(BF16)
    wr_lo = (wr - wr_hi.astype(F32)).astype(BF16)
    x1, hp, idx, tw, counts = _post(
        att.reshape(n, -1), proj.reshape(n, -1), x.reshape(n, -1), mod3, conv_w[0], conv_b[0],
        w_out_conv[0].astype(BF16), w_o_attn[0].astype(BF16), w_merge[0].astype(BF16),
        norm2_g[0], jnp.concatenate([wr_hi, wr_lo], axis=1), b_router[0], seq)

    cnt = counts[0].astype(jnp.int32)
    padded = (cnt + ROW_BLOCK - 1) // ROW_BLOCK * ROW_BLOCK
    pend = jnp.cumsum(padded)
    pstart = pend - padded
    n_blocks = n * TOP_K // ROW_BLOCK + N_EXPERTS
    rows = n_blocks * ROW_BLOCK
    block_start = jnp.arange(n_blocks, dtype=jnp.int32) * ROW_BLOCK
    block_e = jnp.minimum(jnp.sum(pend[None, :] <= block_start[:, None], axis=1),
                          N_EXPERTS - 1).astype(jnp.int32)

    dest_t = _scan(idx, pstart.astype(F32).reshape(1, -1)).T
    xs = _sc_dispatch(dest_t, hp, rows)

    n_used = (pend[-1:] // ROW_BLOCK).astype(jnp.int32)
    changed = jnp.concatenate([jnp.ones((1,), jnp.bool_), block_e[1:] != block_e[:-1]])
    first = (changed & (jnp.arange(n_blocks) < n_used[0])).astype(jnp.int32)
    valid = jnp.clip((pstart + cnt)[block_e] - block_start, 0, ROW_BLOCK).astype(jnp.int32)
    bgu = b_gate_up[0]
    y = _experts(block_e, first, n_used, valid, xs, w_gate_up[0], bgu[:, None, 0::2],
                 bgu[:, None, 1::2], w_down[0], b_down[0][:, None, :])

    n_c = n // COMBINE_CHUNKS
    out = None
    for ch in range(COMBINE_CHUNKS):
        src = dest_t[:, ch * n_c:(ch + 1) * n_c].reshape(1, -1)
        out = _combine(_sc_gather(src, y), x1, tw, mod3, normf_g, seq, ch * n_c // min(TM_POST, seq), out)
    return out.reshape(bsz, seq, D_MODEL)
```

```python
import functools
import math

import jax
import jax.numpy as jnp
from jax import lax
from jax.experimental import pallas as pl
from jax.experimental.pallas import tpu as pltpu
from jax.experimental.pallas import tpu_sc as plsc

F32 = jnp.float32
BF16 = jnp.bfloat16

D_MODEL = 1024
CONV_WIDTH = 512
CONV_K = 3
N_HEADS = 8
HEAD_DIM = 64
V_DIM = 2 * HEAD_DIM
N_BUCKETS = 32
MAX_DISTANCE = 128
N_EXPERTS = 32
TOP_K = 4
SWIGLU_ALPHA = 1.702
SWIGLU_LIMIT = 7.0
EPS = 1e-6
LAM_INIT = 0.8 - 0.6 * math.exp(-0.3 * 0)

D_NAT = 3 * CONV_WIDTH + 3 * D_MODEL
COL_K, COL_GA, COL_GB = 0, 1024, 2048
COL_CB, COL_CC, COL_CU = 3072, 3584, 4096
D_TR = 2 * D_MODEL
ROW_QT, ROW_VT = 0, 1024

TM_IN = 1024
TN_IN = 1536
TQ = 512
HEADS_PER_STEP = 2
TM_POST = 512
T_SCAN = 512
SC_WINDOW = 128
COMBINE_CHUNKS = 4
ROW_BLOCK = 512
MASK_NEG = -1e30
ONES_ROWS = 16
LANES = 128
DEINT_CHUNK = 512
VMEM_LIMIT = 56 * 1024 * 1024


def _sigmoid(x):
    return 1.0 / (1.0 + jnp.exp(-x))


def _rms(x, g):
    return x * lax.rsqrt(jnp.mean(x * x, axis=-1, keepdims=True) + EPS) * g


def _pack_halves(v):
    bits = pltpu.bitcast(v.astype(BF16).astype(F32), jnp.uint32)
    w = v.shape[1] // 2
    return (bits[:, w:] & jnp.uint32(0xFFFF0000)) | (bits[:, :w] >> 16)


def _unpack_halves(words):
    return (pltpu.bitcast(words << 16, F32), pltpu.bitcast(words & jnp.uint32(0xFFFF0000), F32))


def _ada_kernel(c_ref, w_ref, b_ref, lq1_ref, lk1_ref, lq2_ref, lk2_ref, mod_ref, lam_ref):
    c = c_ref[...]
    s = c * _sigmoid(c)
    mod_ref[...] = jnp.dot(s, w_ref[...], precision=lax.Precision.HIGHEST,
                           preferred_element_type=F32) + b_ref[...]
    a1 = jnp.sum(lq1_ref[...] * lk1_ref[...], axis=-1, keepdims=True)
    a2 = jnp.sum(lq2_ref[...] * lk2_ref[...], axis=-1, keepdims=True)
    lam_ref[...] = jnp.broadcast_to(jnp.exp(a1) - jnp.exp(a2) + LAM_INIT, lam_ref.shape)


def _ada(c, w_ada, b_ada, lq1, lk1, lq2, lk2):
    bsz = c.shape[0]
    n_chunks = w_ada.shape[1] // D_MODEL
    vec = pl.BlockSpec((1, HEAD_DIM), lambda j: (0, 0))
    return pl.pallas_call(
        _ada_kernel,
        grid=(n_chunks,),
        in_specs=[pl.BlockSpec((bsz, D_MODEL), lambda j: (0, 0)),
                  pl.BlockSpec((D_MODEL, D_MODEL), lambda j: (0, j)),
                  pl.BlockSpec((1, D_MODEL), lambda j: (0, j)),
                  vec, vec, vec, vec],
        out_specs=[pl.BlockSpec((bsz, D_MODEL), lambda j: (0, j)),
                   pl.BlockSpec((1, 128), lambda j: (0, 0))],
        out_shape=[jax.ShapeDtypeStruct((bsz, n_chunks * D_MODEL), F32),
                   jax.ShapeDtypeStruct((1, 128), F32)],
        name="ada",
    )(c, w_ada, b_ada.reshape(1, -1), lq1.reshape(1, -1), lk1.reshape(1, -1),
      lq2.reshape(1, -1), lk2.reshape(1, -1))


def _inproj_kernel(x_ref, mod_ref, g_ref, w_ref, wt_ref, o_ref, ot_ref, h_scr, ht_scr, *, n_nat):
    j = pl.program_id(2)

    @pl.when(j == 0)
    def _():
        h = _rms(x_ref[0], g_ref[...]) * (1.0 + mod_ref[0, 1:2, :]) + mod_ref[0, 0:1, :]
        h_scr[...] = h.astype(BF16)
        ht_scr[...] = h.T.astype(BF16)

    @pl.when(j < n_nat)
    def _():
        o_ref[0] = jnp.dot(h_scr[...], w_ref[...], preferred_element_type=F32).astype(BF16)

    @pl.when(j == n_nat)
    def _():
        ot_ref[0] = jnp.dot(wt_ref[...], ht_scr[...], preferred_element_type=F32).astype(BF16)


def _inproj(x, mod3, norm1_g, w_nat, w_tr):
    bsz, seq, _ = x.shape
    tm = min(TM_IN, seq)
    n_nat = D_NAT // TN_IN
    nat_j = lambda j: jnp.minimum(j, n_nat - 1)
    return pl.pallas_call(
        functools.partial(_inproj_kernel, n_nat=n_nat),
        grid=(bsz, seq // tm, n_nat + 1),
        in_specs=[pl.BlockSpec((1, tm, D_MODEL), lambda b, i, j: (b, i, 0)),
                  pl.BlockSpec((1, 6, D_MODEL), lambda b, i, j: (b, 0, 0)),
                  pl.BlockSpec((1, D_MODEL), lambda b, i, j: (0, 0)),
                  pl.BlockSpec((D_MODEL, TN_IN), lambda b, i, j: (0, nat_j(j))),
                  pl.BlockSpec((D_TR, D_MODEL), lambda b, i, j: (0, 0))],
        out_specs=[pl.BlockSpec((1, tm, TN_IN), lambda b, i, j: (b, i, nat_j(j))),
                   pl.BlockSpec((1, D_TR, tm), lambda b, i, j: (b, 0, i))],
        out_shape=[jax.ShapeDtypeStruct((bsz, seq, D_NAT), BF16),
                   jax.ShapeDtypeStruct((bsz, D_TR, seq), BF16)],
        scratch_shapes=[pltpu.VMEM((tm, D_MODEL), BF16), pltpu.VMEM((D_MODEL, tm), BF16)],
        compiler_params=pltpu.CompilerParams(
            dimension_semantics=("arbitrary", "arbitrary", "arbitrary"),
            vmem_limit_bytes=VMEM_LIMIT),
        name="inproj",
    )(x, mod3, norm1_g.reshape(1, -1), w_nat, w_tr)


def _attn_kernel(lam_ref, qt_ref, k_ref, vt_ref, bias_ref, g_ref, o_ref,
                 m_scr, acc_scr, vte_scr, s_a, s_b, t_a, t_b, *, tq):
    i = pl.program_id(2)
    heads = range(HEADS_PER_STEP)
    n_chain = 2 * HEADS_PER_STEP
    head_rows = lambda hh: slice(hh * V_DIM, (hh + 1) * V_DIM)
    sub_cols = lambda sub: slice(sub * tq, (sub + 1) * tq)

    @pl.when(i == 0)
    def _():
        for hh in heads:
            vte_scr[hh, 0:V_DIM, :] = vt_ref[0, head_rows(hh), :]
            vte_scr[hh, V_DIM:, :] = jnp.ones((ONES_ROWS, vte_scr.shape[2]), BF16)

    qt_maps = []
    for sub in range(2):
        maps = []
        for hh in heads:
            qt = qt_ref[0, head_rows(hh), sub_cols(sub)] * jnp.asarray(HEAD_DIM ** -0.5, BF16)
            row = lax.broadcasted_iota(jnp.int32, qt.shape, 0)
            zero = jnp.zeros_like(qt)
            maps += [jnp.where(row < HEAD_DIM, qt, zero), jnp.where(row >= HEAD_DIM, qt, zero)]
        qt_maps.append(maps)

    m_scr[...] = jnp.full(m_scr.shape, MASK_NEG, F32)
    acc_scr[...] = jnp.zeros(acc_scr.shape, F32)
    slots = ((s_a, t_a), (s_b, t_b))

    def produce(sub, j, bias, slot):
        s_ref, t_ref = slots[slot]
        for hh in heads:
            k = k_ref[0, pl.ds(pl.multiple_of(j * tq, tq), tq), head_rows(hh)]
            for m in range(2):
                c = 2 * hh + m
                s = jnp.dot(k, qt_maps[sub][c], preferred_element_type=F32)
                if bias is not None:
                    s = s + bias_ref[hh, bias]
                s_ref[c] = s
                t_ref[c] = jnp.max(s, axis=0, keepdims=True)

    def consume(sub, j, slot):
        s_ref, t_ref = slots[slot]
        for hh in heads:
            vte = vte_scr[hh, :, pl.ds(pl.multiple_of(j * tq, tq), tq)]
            for m in range(2):
                c = 2 * hh + m
                a = sub * n_chain + c
                m_prev = m_scr[a]
                m_new = jnp.maximum(m_prev, t_ref[c])
                alpha = jnp.exp(m_prev - m_new)
                p = jnp.exp(s_ref[c] - m_new).astype(BF16)
                acc_scr[a] = alpha * acc_scr[a] + jnp.dot(vte, p, preferred_element_type=F32)
                m_scr[a] = m_new

    diag, off1 = 0, 1
    qa, qb = 0, 1

    @pl.when(i == 0)
    def _():
        produce(qa, 0, diag, 0)
        produce(qb, 0, off1, 1)
        consume(qa, 0, 0)
        produce(qb, 1, diag, 0)
        consume(qb, 0, 1)
        consume(qb, 1, 0)

    @pl.when(i >= 1)
    def _():
        ia = 2 * i
        ib = ia + 1
        produce(qa, 0, None, 0)

        def pair_a(r, carry):
            t = 2 * r
            produce(qa, t + 1, None, 1)
            consume(qa, t, 0)
            produce(qa, t + 2, None, 0)
            consume(qa, t + 1, 1)
            return carry

        lax.fori_loop(0, i - 1, pair_a, 0)
        produce(qa, ia - 1, off1, 1)
        consume(qa, ia - 2, 0)
        produce(qa, ia, diag, 0)
        consume(qa, ia - 1, 1)
        produce(qb, 0, None, 1)
        consume(qa, ia, 0)

        def pair_b(r, carry):
            t = 2 * r
            produce(qb, t + 1, None, 0)
            consume(qb, t, 1)
            produce(qb, t + 2, None, 1)
            consume(qb, t + 1, 0)
            return carry

        lax.fori_loop(0, i - 1, pair_b, 0)
        produce(qb, ib - 2, None, 0)
        consume(qb, ib - 3, 1)
        produce(qb, ib - 1, off1, 1)
        consume(qb, ib - 2, 0)
        produce(qb, ib, diag, 0)
        consume(qb, ib - 1, 1)
        consume(qb, ib, 0)

    lam = lam_ref[0]
    for sub in range(2):
        for hh in heads:
            a1, a2 = acc_scr[sub * n_chain + 2 * hh], acc_scr[sub * n_chain + 2 * hh + 1]
            ot = a1[:V_DIM] / a1[V_DIM:V_DIM + 1] - lam * (a2[:V_DIM] / a2[V_DIM:V_DIM + 1])
            ot = ot * lax.rsqrt(jnp.mean(ot * ot, axis=0, keepdims=True) + EPS) * g_ref[...]
            o_ref[0, sub_cols(sub), head_rows(hh)] = (ot * (1.0 - LAM_INIT)).T.astype(BF16)


def _attention(proj, proj_t, lam, bias_tiles, subln_g):
    bsz, seq, _ = proj.shape
    tq = min(TQ, seq // 2)
    n_pairs = seq // (2 * tq)
    hps = HEADS_PER_STEP
    width = hps * V_DIM
    kern = functools.partial(_attn_kernel, tq=tq)
    score = pltpu.VMEM((2 * hps, tq, tq), F32)
    colmax = pltpu.VMEM((2 * hps, 1, tq), F32)
    return pl.pallas_call(
        kern,
        grid=(bsz, N_HEADS // hps, n_pairs),
        in_specs=[pl.BlockSpec(memory_space=pltpu.SMEM),
                  pl.BlockSpec((1, width, 2 * tq), lambda b, h, i: (b, ROW_QT // width + h, i)),
                  pl.BlockSpec((1, seq, width), lambda b, h, i: (b, 0, COL_K // width + h)),
                  pl.BlockSpec((1, width, seq), lambda b, h, i: (b, ROW_VT // width + h, 0)),
                  pl.BlockSpec((hps, 2, tq, tq), lambda b, h, i: (h, 0, 0, 0)),
                  pl.BlockSpec((V_DIM, 1), lambda b, h, i: (0, 0))],
        out_specs=pl.BlockSpec((1, 2 * tq, width), lambda b, h, i: (b, i, h)),
        out_shape=jax.ShapeDtypeStruct((bsz, seq, N_HEADS * V_DIM), BF16),
        scratch_shapes=[pltpu.VMEM((4 * hps, 1, tq), F32),
                        pltpu.VMEM((4 * hps, V_DIM + ONES_ROWS, tq), F32),
                        pltpu.VMEM((hps, V_DIM + ONES_ROWS, seq), BF16),
                        score, score, colmax, colmax],
        compiler_params=pltpu.CompilerParams(
            dimension_semantics=("arbitrary", "arbitrary", "arbitrary"),
            vmem_limit_bytes=VMEM_LIMIT),
        name="attn",
    )(lam, proj_t, proj, proj_t, bias_tiles, subln_g.reshape(-1, 1))


def _t5_bucket(dist):
    max_exact = N_BUCKETS // 2
    d = jnp.maximum(dist, 1).astype(F32)
    large = max_exact + (jnp.log(d / max_exact) / math.log(MAX_DISTANCE / max_exact)
                         * (N_BUCKETS - max_exact)).astype(jnp.int32)
    large = jnp.minimum(large, N_BUCKETS - 1)
    return jnp.where(dist < max_exact, dist, large)


def _bias_tiles(rel_table, tq):
    assert tq >= MAX_DISTANCE
    r = jnp.arange(tq, dtype=jnp.int32)
    rel0 = r[None, :] - r[:, None]
    rel = jnp.stack([rel0, rel0 + tq])
    table = (rel_table - rel_table[N_BUCKETS - 1:N_BUCKETS, :]).astype(F32)
    bucket = _t5_bucket(jnp.maximum(rel, 0))
    onehot = (bucket[None] == jnp.arange(N_BUCKETS, dtype=jnp.int32)[:, None, None, None]).astype(F32)
    bias = jnp.einsum('kh,kabc->habc', table, onehot, precision=lax.Precision.HIGHEST)
    return jnp.where((rel >= 0)[None], bias, MASK_NEG)


def _post_kernel(att_ref, ga_ref, gb_ref, cb_ref, cc_ref, cu_ref, x_ref, mod_ref,
                 cw_ref, cbias_ref, woc_ref, woa_ref, wmg_ref, g2_ref, wr_ref, br_ref,
                 x1_ref, hp_ref, idx_ref, tw_ref, cnt_ref,
                 ubuf, carry, *, tm, tiles_per_seq):
    i = pl.program_id(0)

    @pl.when(i % tiles_per_seq == 0)
    def _():
        carry[...] = jnp.zeros(carry.shape, F32)

    @pl.when(i == 0)
    def _():
        cnt_ref[...] = jnp.zeros(cnt_ref.shape, F32)

    u = cc_ref[...].astype(F32) * cu_ref[...].astype(F32)
    ubuf[0:8, :] = carry[...]
    ubuf[8:8 + tm, :] = u
    carry[...] = u[tm - 8:tm, :]
    conv = (cbias_ref[...] + ubuf[6:6 + tm, :] * cw_ref[0:1, :]
            + ubuf[7:7 + tm, :] * cw_ref[1:2, :] + u * cw_ref[2:3, :])
    ya = jnp.dot((cb_ref[...].astype(F32) * conv).astype(BF16), woc_ref[...],
                 preferred_element_type=F32)
    yb = jnp.dot(att_ref[...], woa_ref[...], preferred_element_type=F32)
    merged = _sigmoid(ga_ref[...].astype(F32)) * ya + _sigmoid(gb_ref[...].astype(F32)) * yb
    mix = jnp.dot(merged.astype(BF16), wmg_ref[...], preferred_element_type=F32)
    x1 = x_ref[...] + mod_ref[0, 2:3, :] * mix
    x1_ref[...] = x1

    h2 = _rms(x1, g2_ref[...]) * (1.0 + mod_ref[0, 4:5, :]) + mod_ref[0, 3:4, :]
    h_hi = h2.astype(BF16)
    h_hi32 = h_hi.astype(F32)
    h_lo = (h2 - h_hi32).astype(BF16)

    part = jnp.dot(h_hi, wr_ref[...], preferred_element_type=F32)
    logits = (part[:, :N_EXPERTS] + part[:, N_EXPERTS:]
              + jnp.dot(h_lo, wr_ref[:, :N_EXPERTS], preferred_element_type=F32)
              + br_ref[...])

    lane = lax.broadcasted_iota(jnp.int32, logits.shape, 1)
    work = logits
    sel_l, sel_i = [], []
    member = jnp.zeros(logits.shape, F32)
    for _ in range(TOP_K):
        mx = jnp.max(work, axis=-1, keepdims=True)
        ix = jnp.min(jnp.where(work == mx, lane, N_EXPERTS), axis=-1, keepdims=True)
        hit = lane == ix
        member = member + hit.astype(F32)
        work = jnp.where(hit, -jnp.inf, work)
        sel_l.append(mx)
        sel_i.append(ix)
    ex = [jnp.exp(l - sel_l[0]) for l in sel_l]
    den = ex[0] + ex[1] + ex[2] + ex[3]
    tw_ref[...] = jnp.concatenate([e / den for e in ex], axis=1)
    idx_ref[...] = jnp.concatenate(sel_i, axis=1)
    cnt_ref[...] += jnp.sum(member, axis=0, keepdims=True)

    hp_ref[...] = _pack_halves(h2)


def _post(att, proj, x2, mod3, conv_w, conv_b, woc_b, woa_b, wmg_b, norm2_g, wr_split, b_router, seq):
    n = x2.shape[0]
    tm = min(TM_POST, seq)
    tiles_per_seq = seq // tm
    kern = functools.partial(_post_kernel, tm=tm, tiles_per_seq=tiles_per_seq)
    row = lambda c: (lambda i: (i, c))
    const = lambda i: (0, 0)
    return pl.pallas_call(
        kern,
        grid=(n // tm,),
        in_specs=[pl.BlockSpec((tm, D_MODEL), row(0)),
                  pl.BlockSpec((tm, D_MODEL), row(COL_GA // D_MODEL)),
                  pl.BlockSpec((tm, D_MODEL), row(COL_GB // D_MODEL)),
                  pl.BlockSpec((tm, CONV_WIDTH), row(COL_CB // CONV_WIDTH)),
                  pl.BlockSpec((tm, CONV_WIDTH), row(COL_CC // CONV_WIDTH)),
                  pl.BlockSpec((tm, CONV_WIDTH), row(COL_CU // CONV_WIDTH)),
                  pl.BlockSpec((tm, D_MODEL), row(0)),
                  pl.BlockSpec((1, 6, D_MODEL), lambda i: (i // tiles_per_seq, 0, 0)),
                  pl.BlockSpec((CONV_K, CONV_WIDTH), const),
                  pl.BlockSpec((1, CONV_WIDTH), const),
                  pl.BlockSpec((CONV_WIDTH, D_MODEL), const),
                  pl.BlockSpec((D_MODEL, D_MODEL), const),
                  pl.BlockSpec((D_MODEL, D_MODEL), const),
                  pl.BlockSpec((1, D_MODEL), const),
                  pl.BlockSpec((D_MODEL, 2 * N_EXPERTS), const),
                  pl.BlockSpec((1, N_EXPERTS), const)],
        out_specs=[pl.BlockSpec((tm, D_MODEL), row(0)),
                   pl.BlockSpec((tm, D_MODEL // 2), row(0)),
                   pl.BlockSpec((tm, TOP_K), row(0)),
                   pl.BlockSpec((tm, TOP_K), row(0)),
                   pl.BlockSpec((1, N_EXPERTS), const)],
        out_shape=[jax.ShapeDtypeStruct((n, D_MODEL), F32),
                   jax.ShapeDtypeStruct((n, D_MODEL // 2), jnp.uint32),
                   jax.ShapeDtypeStruct((n, TOP_K), jnp.int32),
                   jax.ShapeDtypeStruct((n, TOP_K), F32),
                   jax.ShapeDtypeStruct((1, N_EXPERTS), F32)],
        scratch_shapes=[pltpu.VMEM((tm + 8, CONV_WIDTH), F32), pltpu.VMEM((8, CONV_WIDTH), F32)],
        compiler_params=pltpu.CompilerParams(dimension_semantics=("arbitrary",),
                                             vmem_limit_bytes=VMEM_LIMIT),
        name="post",
    )(att, proj, proj, proj, proj, proj, x2, mod3, conv_w, conv_b.reshape(1, -1),
      woc_b, woa_b, wmg_b, norm2_g.reshape(1, -1), wr_split, b_router.reshape(1, -1))


def _scan_kernel(idx_ref, pstart_ref, dest_ref, carry, earlier, *, t):
    @pl.when(pl.program_id(0) == 0)
    def _():
        carry[...] = jnp.zeros(carry.shape, F32)
        r = lax.broadcasted_iota(jnp.int32, (t, t), 0)
        c = lax.broadcasted_iota(jnp.int32, (t, t), 1)
        earlier[...] = jnp.where(c < r, 1.0, 0.0).astype(BF16)

    idx = idx_ref[...]
    lane = lax.broadcasted_iota(jnp.int32, (t, N_EXPERTS), 1)
    hits = [lane == idx[:, k:k + 1] for k in range(TOP_K)]
    member = hits[0].astype(F32)
    for k in range(1, TOP_K):
        member = member + hits[k].astype(F32)
    rank = jnp.dot(earlier[...], member.astype(BF16), preferred_element_type=F32) + carry[...]
    dest_e = pstart_ref[...] + rank
    cols = [jnp.sum(jnp.where(hits[k], dest_e, 0.0), axis=-1, keepdims=True) for k in range(TOP_K)]
    dest_ref[...] = jnp.concatenate(cols, axis=1).astype(jnp.int32)
    carry[...] += jnp.sum(member, axis=0, keepdims=True)


def _scan(idx, pstart):
    n = idx.shape[0]
    t = min(T_SCAN, n)
    return pl.pallas_call(
        functools.partial(_scan_kernel, t=t),
        grid=(n // t,),
        in_specs=[pl.BlockSpec((t, TOP_K), lambda i: (i, 0)),
                  pl.BlockSpec((1, N_EXPERTS), lambda i: (0, 0))],
        out_specs=pl.BlockSpec((t, TOP_K), lambda i: (i, 0)),
        out_shape=jax.ShapeDtypeStruct((n, TOP_K), jnp.int32),
        scratch_shapes=[pltpu.VMEM((1, N_EXPERTS), F32), pltpu.VMEM((t, t), BF16)],
        compiler_params=pltpu.CompilerParams(dimension_semantics=("arbitrary",)),
        name="scan",
    )(idx, pstart)


def _sc_mesh():
    return plsc.VectorSubcoreMesh(core_axis_name="core", subcore_axis_name="subcore")


def _sc_worker(mesh):
    return lax.axis_index("core") * mesh.num_subcores + lax.axis_index("subcore")


def _sc_dispatch(dest_t, hp, rows):
    n, width = hp.shape
    mesh = _sc_mesh()
    workers = mesh.num_cores * mesh.num_subcores
    per_worker = n // (SC_WINDOW * workers)
    assert per_worker * SC_WINDOW * workers == n

    @pl.kernel(out_type=jax.ShapeDtypeStruct((rows, width), hp.dtype), mesh=mesh,
               scratch_types=[pltpu.VMEM((SC_WINDOW, width), hp.dtype),
                              pltpu.VMEM((TOP_K, SC_WINDOW), jnp.int32)])
    def scatter(hp_hbm, dest_hbm, xs_hbm, row_buf, idx_buf):
        first = _sc_worker(mesh) * per_worker

        @pl.loop(0, per_worker)
        def _(w):
            t0 = (first + w) * SC_WINDOW
            pltpu.sync_copy(hp_hbm.at[pl.ds(t0, SC_WINDOW)], row_buf)
            pltpu.sync_copy(dest_hbm.at[:, pl.ds(t0, SC_WINDOW)], idx_buf)
            for k in range(TOP_K):
                pltpu.sync_copy(row_buf, xs_hbm.at[idx_buf.at[k]])

    return scatter(hp, dest_t)


def _sc_gather(src_rows, table):
    n_out = src_rows.shape[1]
    width = table.shape[1]
    mesh = _sc_mesh()
    workers = mesh.num_cores * mesh.num_subcores
    per_worker = n_out // (SC_WINDOW * workers)
    assert per_worker * SC_WINDOW * workers == n_out

    @pl.kernel(out_type=jax.ShapeDtypeStruct((n_out, width), table.dtype), mesh=mesh,
               scratch_types=[pltpu.VMEM((SC_WINDOW, width), table.dtype),
                              pltpu.VMEM((1, SC_WINDOW), jnp.int32)])
    def gather(table_hbm, src_hbm, out_hbm, row_buf, idx_buf):
        first = _sc_worker(mesh) * per_worker

        @pl.loop(0, per_worker)
        def _(w):
            r0 = (first + w) * SC_WINDOW
            pltpu.sync_copy(src_hbm.at[:, pl.ds(r0, SC_WINDOW)], idx_buf)
            pltpu.sync_copy(table_hbm.at[idx_buf.at[0]], row_buf)
            pltpu.sync_copy(row_buf, out_hbm.at[pl.ds(r0, SC_WINDOW)])

    return gather(table, src_rows)


def _expert_kernel(be_ref, first_ref, used_ref, valid_ref, xs_ref, wgu_ref, bg_ref, bl_ref, wd_ref,
                   bd_ref, y_ref, wg_s, wl_s, wd_s, tmp_s):
    del be_ref
    i = pl.program_id(0)

    @pl.when(first_ref[i] == 1)
    def _():
        half = DEINT_CHUNK // 2
        for c in range(wgu_ref.shape[2] // DEINT_CHUNK):
            for db in range(D_MODEL // LANES):
                rows = slice(db * LANES, (db + 1) * LANES)
                tmp_s[...] = wgu_ref[0, rows, c * DEINT_CHUNK:(c + 1) * DEINT_CHUNK].T
                feats = slice(c * half, (c + 1) * half)
                wg_s[feats, rows] = tmp_s[pl.ds(0, half, stride=2), :].astype(BF16)
                wl_s[feats, rows] = tmp_s[pl.ds(1, half, stride=2), :].astype(BF16)
        wd_s[...] = wd_ref[0].astype(BF16)

    @pl.when(i < used_ref[0])
    def _():
        xp = xs_ref[...]
        rid = lax.broadcasted_iota(jnp.int32, xp.shape, 0)
        xp = jnp.where(rid < valid_ref[i], xp, jnp.uint32(0))
        xb = jnp.concatenate(_unpack_halves(xp), axis=1).astype(BF16)
        nt = (((1,), (1,)), ((), ()))
        g = lax.dot_general(xb, wg_s[...], nt, preferred_element_type=F32) + bg_ref[0]
        l = lax.dot_general(xb, wl_s[...], nt, preferred_element_type=F32) + bl_ref[0]
        glu = jnp.minimum(g, SWIGLU_LIMIT)
        lin = jnp.clip(l, -SWIGLU_LIMIT, SWIGLU_LIMIT)
        act = glu * _sigmoid(SWIGLU_ALPHA * glu) * (lin + 1.0)
        y = jnp.dot(act.astype(BF16), wd_s[...], preferred_element_type=F32) + bd_ref[0]
        y_ref[...] = _pack_halves(y)

    @pl.when(i >= used_ref[0])
    def _():
        y_ref[...] = jnp.zeros(y_ref.shape, y_ref.dtype)


def _experts(block_e, first, n_used, valid, xs, wgu, bg, bl, wd, bd):
    rows, width = xs.shape
    d_exp = wd.shape[1]
    n_blocks = rows // ROW_BLOCK
    wsel = lambda i, be, fi, nu, va: (be[i], 0, 0)
    rsel = lambda i, be, fi, nu, va: (jnp.minimum(i, nu[0] - 1), 0)
    grid_spec = pltpu.PrefetchScalarGridSpec(
        num_scalar_prefetch=4,
        grid=(n_blocks,),
        in_specs=[pl.BlockSpec((ROW_BLOCK, width), rsel),
                  pl.BlockSpec((1, D_MODEL, 2 * d_exp), wsel),
                  pl.BlockSpec((1, 1, d_exp), wsel),
                  pl.BlockSpec((1, 1, d_exp), wsel),
                  pl.BlockSpec((1, d_exp, D_MODEL), wsel),
                  pl.BlockSpec((1, 1, D_MODEL), wsel)],
        out_specs=pl.BlockSpec((ROW_BLOCK, width), lambda i, be, fi, nu, va: (i, 0)),
        scratch_shapes=[pltpu.VMEM((d_exp, D_MODEL), BF16), pltpu.VMEM((d_exp, D_MODEL), BF16),
                        pltpu.VMEM((d_exp, D_MODEL), BF16), pltpu.VMEM((DEINT_CHUNK, LANES), F32)],
    )
    return pl.pallas_call(
        _expert_kernel,
        grid_spec=grid_spec,
        out_shape=jax.ShapeDtypeStruct((rows, width), xs.dtype),
        compiler_params=pltpu.CompilerParams(dimension_semantics=("arbitrary",),
                                             vmem_limit_bytes=VMEM_LIMIT),
        name="experts",
    )(block_e, first, n_used, valid, xs, wgu, bg, bl, wd, bd)


def _combine_kernel(y0_ref, y1_ref, y2_ref, y3_ref, x1_ref, tw_ref, mod_ref, gf_ref, *rest):
    o_ref = rest[-1]
    tw = tw_ref[...]
    lo = hi = None
    for k, yk_ref in enumerate((y0_ref, y1_ref, y2_ref, y3_ref)):
        y_lo, y_hi = _unpack_halves(yk_ref[...])
        w = tw[:, k:k + 1]
        lo = w * y_lo if lo is None else lo + w * y_lo
        hi = w * y_hi if hi is None else hi + w * y_hi
    moe = jnp.concatenate([lo, hi], axis=1)
    xo = x1_ref[...] + mod_ref[0, 5:6, :] * moe
    o_ref[...] = _rms(xo, gf_ref[...])


def _combine(yk, x1, tw, mod3, normf_g, seq, tile0, prev):
    n = x1.shape[0]
    t = min(TM_POST, seq)
    tiles_per_seq = seq // t
    n_c = yk.shape[0] // TOP_K
    slot = lambda k: pl.BlockSpec((t, yk.shape[1]), lambda i: (k * (n_c // t) + i, 0))
    tok = lambda w: pl.BlockSpec((t, w), lambda i: (tile0 + i, 0))
    in_specs = [slot(0), slot(1), slot(2), slot(3), tok(D_MODEL), tok(TOP_K),
                pl.BlockSpec((1, 6, D_MODEL), lambda i: ((tile0 + i) // tiles_per_seq, 0, 0)),
                pl.BlockSpec((1, D_MODEL), lambda i: (0, 0))]
    args = [yk, yk, yk, yk, x1, tw, mod3, normf_g.reshape(1, -1)]
    aliases = {}
    if prev is not None:
        aliases = {len(args): 0}
        in_specs.append(pl.BlockSpec(memory_space=pl.ANY))
        args.append(prev)
    return pl.pallas_call(
        _combine_kernel,
        grid=(n_c // t,),
        in_specs=in_specs,
        out_specs=tok(D_MODEL),
        out_shape=jax.ShapeDtypeStruct((n, D_MODEL), F32),
        input_output_aliases=aliases,
        compiler_params=pltpu.CompilerParams(dimension_semantics=("arbitrary",)),
        name="combine",
    )(*args)


def _in_proj_weights(w_in):
    c0 = 3 * CONV_WIDTH
    q, k, v = (w_in[:, c0 + s * D_MODEL:c0 + (s + 1) * D_MODEL] for s in range(3))
    w_nat = jnp.concatenate([k, w_in[:, c0 + 3 * D_MODEL:], w_in[:, :c0]], axis=1).astype(BF16)
    w_tr = jnp.concatenate([q, v], axis=1).T.astype(BF16)
    return w_nat, w_tr


def kernel(x, c, w_ada, b_ada, norm1_g, norm2_g, w_in, conv_w, conv_b, w_out_conv, lambda_q1, lambda_k1, lambda_q2, lambda_k2, subln_g, w_o_attn, w_merge, w_router, b_router, w_gate_up, b_gate_up, w_down, b_down, rel_bias_table, normf_g):
    bsz, seq, _ = x.shape
    n = bsz * seq
    assert w_ada.shape[0] == 1, "single layer"
    assert seq % (2 * min(TQ, seq // 2)) == 0 and n % TM_POST == 0 and (n * TOP_K) % ROW_BLOCK == 0

    mod, lam = _ada(c, w_ada[0], b_ada[0], lambda_q1[0], lambda_k1[0], lambda_q2[0], lambda_k2[0])
    mod3 = mod.reshape(bsz, 6, D_MODEL)

    proj, proj_t = _inproj(x, mod3, norm1_g[0], *_in_proj_weights(w_in[0]))
    att = _attention(proj, proj_t, lam[0, :1], _bias_tiles(rel_bias_table, min(TQ, seq // 2)), subln_g[0])

    wr = w_router[0]
    wr_hi = wr.astype(BF16)
    wr_lo = (wr - wr_hi.astype(F32)).astype(BF16)
    x1, hp, idx, tw, counts = _post(
        att.reshape(n, -1), proj.reshape(n, -1), x.reshape(n, -1), mod3, conv_w[0], conv_b[0],
        w_out_conv[0].astype(BF16), w_o_attn[0].astype(BF16), w_merge[0].astype(BF16),
        norm2_g[0], jnp.concatenate([wr_hi, wr_lo], axis=1), b_router[0], seq)

    cnt = counts[0].astype(jnp.int32)
    padded = (cnt + ROW_BLOCK - 1) // ROW_BLOCK * ROW_BLOCK
    pend = jnp.cumsum(padded)
    pstart = pend - padded
    n_blocks = n * TOP_K // ROW_BLOCK + N_EXPERTS
    rows = n_blocks * ROW_BLOCK
    block_start = jnp.arange(n_blocks, dtype=jnp.int32) * ROW_BLOCK
    block_e = jnp.minimum(jnp.sum(pend[None, :] <= block_start[:, None], axis=1),
                          N_EXPERTS - 1).astype(jnp.int32)

    dest_t = _scan(idx, pstart.astype(F32).reshape(1, -1)).T
    xs = _sc_dispatch(dest_t, hp, rows)

    n_used = (pend[-1:] // ROW_BLOCK).astype(jnp.int32)
    changed = jnp.concatenate([jnp.ones((1,), jnp.bool_), block_e[1:] != block_e[:-1]])
    first = (changed & (jnp.arange(n_blocks) < n_used[0])).astype(jnp.int32)
    valid = jnp.clip((pstart + cnt)[block_e] - block_start, 0, ROW_BLOCK).astype(jnp.int32)
    bgu = b_gate_up[0]
    y = _experts(block_e, first, n_used, valid, xs, w_gate_up[0], bgu[:, None, 0::2],
                 bgu[:, None, 1::2], w_down[0], b_down[0][:, None, :])

    n_c = n // COMBINE_CHUNKS
    out = None
    for ch in range(COMBINE_CHUNKS):
        src = dest_t[:, ch * n_c:(ch + 1) * n_c].reshape(1, -1)
        out = _combine(_sc_gather(src, y), x1, tw, mod3, normf_g, seq, ch * n_c // min(TM_POST, seq), out)
    return out.reshape(bsz, seq, D_MODEL)
```

```python
import functools
import math

import jax
import jax.numpy as jnp
from jax import lax
from jax.experimental import pallas as pl
from jax.experimental.pallas import tpu as pltpu
from jax.experimental.pallas import tpu_sc as plsc

F32 = jnp.float32
BF16 = jnp.bfloat16

D_MODEL = 1024
CONV_WIDTH = 512
CONV_K = 3
N_HEADS = 8
HEAD_DIM = 64
V_DIM = 2 * HEAD_DIM
N_BUCKETS = 32
MAX_DISTANCE = 128
N_EXPERTS = 32
TOP_K = 4
SWIGLU_ALPHA = 1.702
SWIGLU_LIMIT = 7.0
EPS = 1e-6
LAM_INIT = 0.8 - 0.6 * math.exp(-0.3 * 0)

D_NAT = 3 * CONV_WIDTH + 3 * D_MODEL
COL_K, COL_GA, COL_GB = 0, 1024, 2048
COL_CB, COL_CC, COL_CU = 3072, 3584, 4096
D_TR = 2 * D_MODEL
ROW_QT, ROW_VT = 0, 1024

TM_IN = 1024
TN_IN = 1536
TQ = 512
HEADS_PER_STEP = 2
TM_POST = 512
T_SCAN = 512
SC_WINDOW = 128
COMBINE_CHUNKS = 2
ROW_BLOCK = 512
MASK_NEG = -1e30
ONES_ROWS = 16
LANES = 128
DEINT_CHUNK = 512
VMEM_LIMIT = 56 * 1024 * 1024


def _sigmoid(x):
    return 1.0 / (1.0 + jnp.exp(-x))


def _rms(x, g):
    return x * lax.rsqrt(jnp.mean(x * x, axis=-1, keepdims=True) + EPS) * g


def _pack_halves(v):
    bits = pltpu.bitcast(v.astype(BF16).astype(F32), jnp.uint32)
    w = v.shape[1] // 2
    return (bits[:, w:] & jnp.uint32(0xFFFF0000)) | (bits[:, :w] >> 16)


def _unpack_halves(words):
    return (pltpu.bitcast(words << 16, F32), pltpu.bitcast(words & jnp.uint32(0xFFFF0000), F32))


def _ada_kernel(c_ref, w_ref, b_ref, lq1_ref, lk1_ref, lq2_ref, lk2_ref, mod_ref, lam_ref):
    c = c_ref[...]
    s = c * _sigmoid(c)
    mod_ref[...] = jnp.dot(s, w_ref[...], precision=lax.Precision.HIGHEST,
                           preferred_element_type=F32) + b_ref[...]
    a1 = jnp.sum(lq1_ref[...] * lk1_ref[...], axis=-1, keepdims=True)
    a2 = jnp.sum(lq2_ref[...] * lk2_ref[...], axis=-1, keepdims=True)
    lam_ref[...] = jnp.broadcast_to(jnp.exp(a1) - jnp.exp(a2) + LAM_INIT, lam_ref.shape)


def _ada(c, w_ada, b_ada, lq1, lk1, lq2, lk2):
    bsz = c.shape[0]
    n_chunks = w_ada.shape[1] // D_MODEL
    vec = pl.BlockSpec((1, HEAD_DIM), lambda j: (0, 0))
    return pl.pallas_call(
        _ada_kernel,
        grid=(n_chunks,),
        in_specs=[pl.BlockSpec((bsz, D_MODEL), lambda j: (0, 0)),
                  pl.BlockSpec((D_MODEL, D_MODEL), lambda j: (0, j)),
                  pl.BlockSpec((1, D_MODEL), lambda j: (0, j)),
                  vec, vec, vec, vec],
        out_specs=[pl.BlockSpec((bsz, D_MODEL), lambda j: (0, j)),
                   pl.BlockSpec((1, 128), lambda j: (0, 0))],
        out_shape=[jax.ShapeDtypeStruct((bsz, n_chunks * D_MODEL), F32),
                   jax.ShapeDtypeStruct((1, 128), F32)],
        name="ada",
    )(c, w_ada, b_ada.reshape(1, -1), lq1.reshape(1, -1), lk1.reshape(1, -1),
      lq2.reshape(1, -1), lk2.reshape(1, -1))


def _inproj_kernel(x_ref, mod_ref, g_ref, w_ref, wt_ref, o_ref, ot_ref, h_scr, *, n_nat):
    j = pl.program_id(2)

    @pl.when(j == 0)
    def _():
        h = _rms(x_ref[0], g_ref[...]) * (1.0 + mod_ref[0, 1:2, :]) + mod_ref[0, 0:1, :]
        h_scr[...] = h.astype(BF16)

    @pl.when(j < n_nat)
    def _():
        o_ref[0] = jnp.dot(h_scr[...], w_ref[...], preferred_element_type=F32).astype(BF16)

    @pl.when(j == n_nat)
    def _():
        ot_ref[0] = lax.dot_general(wt_ref[...], h_scr[...], (((1,), (1,)), ((), ())),
                                    preferred_element_type=F32).astype(BF16)


def _inproj(x, mod3, norm1_g, w_nat, w_tr):
    bsz, seq, _ = x.shape
    tm = min(TM_IN, seq)
    n_nat = D_NAT // TN_IN
    nat_j = lambda j: jnp.minimum(j, n_nat - 1)
    return pl.pallas_call(
        functools.partial(_inproj_kernel, n_nat=n_nat),
        grid=(bsz, seq // tm, n_nat + 1),
        in_specs=[pl.BlockSpec((1, tm, D_MODEL), lambda b, i, j: (b, i, 0)),
                  pl.BlockSpec((1, 6, D_MODEL), lambda b, i, j: (b, 0, 0)),
                  pl.BlockSpec((1, D_MODEL), lambda b, i, j: (0, 0)),
                  pl.BlockSpec((D_MODEL, TN_IN), lambda b, i, j: (0, nat_j(j))),
                  pl.BlockSpec((D_TR, D_MODEL), lambda b, i, j: (0, 0))],
        out_specs=[pl.BlockSpec((1, tm, TN_IN), lambda b, i, j: (b, i, nat_j(j))),
                   pl.BlockSpec((1, D_TR, tm), lambda b, i, j: (b, 0, i))],
        out_shape=[jax.ShapeDtypeStruct((bsz, seq, D_NAT), BF16),
                   jax.ShapeDtypeStruct((bsz, D_TR, seq), BF16)],
        scratch_shapes=[pltpu.VMEM((tm, D_MODEL), BF16)],
        compiler_params=pltpu.CompilerParams(
            dimension_semantics=("arbitrary", "arbitrary", "arbitrary"),
            vmem_limit_bytes=VMEM_LIMIT),
        name="inproj",
    )(x, mod3, norm1_g.reshape(1, -1), w_nat, w_tr)


def _attn_kernel(lam_ref, qt_ref, k_ref, vt_ref, bias_ref, g_ref, o_ref,
                 m_scr, acc_scr, vte_scr, s_a, s_b, t_a, t_b, *, tq):
    i = pl.program_id(2)
    heads = range(HEADS_PER_STEP)
    n_chain = 2 * HEADS_PER_STEP
    head_rows = lambda hh: slice(hh * V_DIM, (hh + 1) * V_DIM)
    sub_cols = lambda sub: slice(sub * tq, (sub + 1) * tq)

    @pl.when(i == 0)
    def _():
        for hh in heads:
            vte_scr[hh, 0:V_DIM, :] = vt_ref[0, head_rows(hh), :]
            vte_scr[hh, V_DIM:, :] = jnp.ones((ONES_ROWS, vte_scr.shape[2]), BF16)

    qt_maps = []
    for sub in range(2):
        maps = []
        for hh in heads:
            qt = qt_ref[0, head_rows(hh), sub_cols(sub)] * jnp.asarray(HEAD_DIM ** -0.5, BF16)
            row = lax.broadcasted_iota(jnp.int32, qt.shape, 0)
            zero = jnp.zeros_like(qt)
            maps += [jnp.where(row < HEAD_DIM, qt, zero), jnp.where(row >= HEAD_DIM, qt, zero)]
        qt_maps.append(maps)

    m_scr[...] = jnp.full(m_scr.shape, MASK_NEG, F32)
    acc_scr[...] = jnp.zeros(acc_scr.shape, F32)
    slots = ((s_a, t_a), (s_b, t_b))

    def produce(sub, j, bias, slot):
        s_ref, t_ref = slots[slot]
        for hh in heads:
            k = k_ref[0, pl.ds(pl.multiple_of(j * tq, tq), tq), head_rows(hh)]
            for m in range(2):
                c = 2 * hh + m
                s = jnp.dot(k, qt_maps[sub][c], preferred_element_type=F32)
                if bias is not None:
                    s = s + bias_ref[hh, bias]
                s_ref[c] = s
                t_ref[c] = jnp.max(s, axis=0, keepdims=True)

    def consume(sub, j, slot):
        s_ref, t_ref = slots[slot]
        for hh in heads:
            vte = vte_scr[hh, :, pl.ds(pl.multiple_of(j * tq, tq), tq)]
            for m in range(2):
                c = 2 * hh + m
                a = sub * n_chain + c
                m_prev = m_scr[a]
                m_new = jnp.maximum(m_prev, t_ref[c])
                alpha = jnp.exp(m_prev - m_new)
                p = jnp.exp(s_ref[c] - m_new).astype(BF16)
                acc_scr[a] = alpha * acc_scr[a] + jnp.dot(vte, p, preferred_element_type=F32)
                m_scr[a] = m_new

    diag, off1 = 0, 1
    qa, qb = 0, 1

    @pl.when(i == 0)
    def _():
        produce(qa, 0, diag, 0)
        produce(qb, 0, off1, 1)
        consume(qa, 0, 0)
        produce(qb, 1, diag, 0)
        consume(qb, 0, 1)
        consume(qb, 1, 0)

    @pl.when(i >= 1)
    def _():
        ia = 2 * i
        ib = ia + 1
        produce(qa, 0, None, 0)

        def pair_a(r, carry):
            t = 2 * r
            produce(qa, t + 1, None, 1)
            consume(qa, t, 0)
            produce(qa, t + 2, None, 0)
            consume(qa, t + 1, 1)
            return carry

        lax.fori_loop(0, i - 1, pair_a, 0)
        produce(qa, ia - 1, off1, 1)
        consume(qa, ia - 2, 0)
        produce(qa, ia, diag, 0)
        consume(qa, ia - 1, 1)
        produce(qb, 0, None, 1)
        consume(qa, ia, 0)

        def pair_b(r, carry):
            t = 2 * r
            produce(qb, t + 1, None, 0)
            consume(qb, t, 1)
            produce(qb, t + 2, None, 1)
            consume(qb, t + 1, 0)
            return carry

        lax.fori_loop(0, i - 1, pair_b, 0)
        produce(qb, ib - 2, None, 0)
        consume(qb, ib - 3, 1)
        produce(qb, ib - 1, off1, 1)
        consume(qb, ib - 2, 0)
        produce(qb, ib, diag, 0)
        consume(qb, ib - 1, 1)
        consume(qb, ib, 0)

    lam = lam_ref[0]
    for sub in range(2):
        for hh in heads:
            a1, a2 = acc_scr[sub * n_chain + 2 * hh], acc_scr[sub * n_chain + 2 * hh + 1]
            ot = a1[:V_DIM] / a1[V_DIM:V_DIM + 1] - lam * (a2[:V_DIM] / a2[V_DIM:V_DIM + 1])
            ot = ot * lax.rsqrt(jnp.mean(ot * ot, axis=0, keepdims=True) + EPS) * g_ref[...]
            o_ref[0, sub_cols(sub), head_rows(hh)] = (ot * (1.0 - LAM_INIT)).T.astype(BF16)


def _attention(proj, proj_t, lam, bias_tiles, subln_g):
    bsz, seq, _ = proj.shape
    tq = min(TQ, seq // 2)
    n_pairs = seq // (2 * tq)
    hps = HEADS_PER_STEP
    width = hps * V_DIM
    kern = functools.partial(_attn_kernel, tq=tq)
    score = pltpu.VMEM((2 * hps, tq, tq), F32)
    colmax = pltpu.VMEM((2 * hps, 1, tq), F32)
    return pl.pallas_call(
        kern,
        grid=(bsz, N_HEADS // hps, n_pairs),
        in_specs=[pl.BlockSpec(memory_space=pltpu.SMEM),
                  pl.BlockSpec((1, width, 2 * tq), lambda b, h, i: (b, ROW_QT // width + h, i)),
                  pl.BlockSpec((1, seq, width), lambda b, h, i: (b, 0, COL_K // width + h)),
                  pl.BlockSpec((1, width, seq), lambda b, h, i: (b, ROW_VT // width + h, 0)),
                  pl.BlockSpec((hps, 2, tq, tq), lambda b, h, i: (h, 0, 0, 0)),
                  pl.BlockSpec((V_DIM, 1), lambda b, h, i: (0, 0))],
        out_specs=pl.BlockSpec((1, 2 * tq, width), lambda b, h, i: (b, i, h)),
        out_shape=jax.ShapeDtypeStruct((bsz, seq, N_HEADS * V_DIM), BF16),
        scratch_shapes=[pltpu.VMEM((4 * hps, 1, tq), F32),
                        pltpu.VMEM((4 * hps, V_DIM + ONES_ROWS, tq), F32),
                        pltpu.VMEM((hps, V_DIM + ONES_ROWS, seq), BF16),
                        score, score, colmax, colmax],
        compiler_params=pltpu.CompilerParams(
            dimension_semantics=("arbitrary", "arbitrary", "arbitrary"),
            vmem_limit_bytes=VMEM_LIMIT),
        name="attn",
    )(lam, proj_t, proj, proj_t, bias_tiles, subln_g.reshape(-1, 1))


def _t5_bucket(dist):
    max_exact = N_BUCKETS // 2
    d = jnp.maximum(dist, 1).astype(F32)
    large = max_exact + (jnp.log(d / max_exact) / math.log(MAX_DISTANCE / max_exact)
                         * (N_BUCKETS - max_exact)).astype(jnp.int32)
    large = jnp.minimum(large, N_BUCKETS - 1)
    return jnp.where(dist < max_exact, dist, large)


def _bias_tiles(rel_table, tq):
    assert tq >= MAX_DISTANCE
    r = jnp.arange(tq, dtype=jnp.int32)
    rel0 = r[None, :] - r[:, None]
    rel = jnp.stack([rel0, rel0 + tq])
    table = (rel_table - rel_table[N_BUCKETS - 1:N_BUCKETS, :]).astype(F32)
    bucket = _t5_bucket(jnp.maximum(rel, 0))
    onehot = (bucket[None] == jnp.arange(N_BUCKETS, dtype=jnp.int32)[:, None, None, None]).astype(F32)
    bias = jnp.einsum('kh,kabc->habc', table, onehot, precision=lax.Precision.HIGHEST)
    return jnp.where((rel >= 0)[None], bias, MASK_NEG)


def _post_kernel(att_ref, ga_ref, gb_ref, cb_ref, cc_ref, cu_ref, x_ref, mod_ref,
                 cw_ref, cbias_ref, woc_ref, woa_ref, wmg_ref, g2_ref, wr_ref, br_ref,
                 x1_ref, hp_ref, idx_ref, tw_ref, cnt_ref,
                 ubuf, carry, *, tm, tiles_per_seq):
    i = pl.program_id(0)

    @pl.when(i % tiles_per_seq == 0)
    def _():
        carry[...] = jnp.zeros(carry.shape, F32)

    @pl.when(i == 0)
    def _():
        cnt_ref[...] = jnp.zeros(cnt_ref.shape, F32)

    u = cc_ref[...].astype(F32) * cu_ref[...].astype(F32)
    ubuf[0:8, :] = carry[...]
    ubuf[8:8 + tm, :] = u
    carry[...] = u[tm - 8:tm, :]
    conv = (cbias_ref[...] + ubuf[6:6 + tm, :] * cw_ref[0:1, :]
            + ubuf[7:7 + tm, :] * cw_ref[1:2, :] + u * cw_ref[2:3, :])
    ya = jnp.dot((cb_ref[...].astype(F32) * conv).astype(BF16), woc_ref[...],
                 preferred_element_type=F32)
    yb = jnp.dot(att_ref[...], woa_ref[...], preferred_element_type=F32)
    merged = _sigmoid(ga_ref[...].astype(F32)) * ya + _sigmoid(gb_ref[...].astype(F32)) * yb
    mix = jnp.dot(merged.astype(BF16), wmg_ref[...], preferred_element_type=F32)
    x1 = x_ref[...] + mod_ref[0, 2:3, :] * mix
    x1_ref[...] = x1

    h2 = _rms(x1, g2_ref[...]) * (1.0 + mod_ref[0, 4:5, :]) + mod_ref[0, 3:4, :]
    h_hi = h2.astype(BF16)
    h_hi32 = h_hi.astype(F32)
    h_lo = (h2 - h_hi32).astype(BF16)

    part = jnp.dot(h_hi, wr_ref[...], preferred_element_type=F32)
    logits = (part[:, :N_EXPERTS] + part[:, N_EXPERTS:]
              + jnp.dot(h_lo, wr_ref[:, :N_EXPERTS], preferred_element_type=F32)
              + br_ref[...])

    lane = lax.broadcasted_iota(jnp.int32, logits.shape, 1)
    work = logits
    sel_l, sel_i = [], []
    member = jnp.zeros(logits.shape, F32)
    for _ in range(TOP_K):
        mx = jnp.max(work, axis=-1, keepdims=True)
        ix = jnp.min(jnp.where(work == mx, lane, N_EXPERTS), axis=-1, keepdims=True)
        hit = lane == ix
        member = member + hit.astype(F32)
        work = jnp.where(hit, -jnp.inf, work)
        sel_l.append(mx)
        sel_i.append(ix)
    ex = [jnp.exp(l - sel_l[0]) for l in sel_l]
    den = ex[0] + ex[1] + ex[2] + ex[3]
    tw_ref[...] = jnp.concatenate([e / den for e in ex], axis=1)
    idx_ref[...] = jnp.concatenate(sel_i, axis=1)
    cnt_ref[...] += jnp.sum(member, axis=0, keepdims=True)

    hp_ref[...] = _pack_halves(h2)


def _post(att, proj, x2, mod3, conv_w, conv_b, woc_b, woa_b, wmg_b, norm2_g, wr_split, b_router, seq):
    n = x2.shape[0]
    tm = min(TM_POST, seq)
    tiles_per_seq = seq // tm
    kern = functools.partial(_post_kernel, tm=tm, tiles_per_seq=tiles_per_seq)
    row = lambda c: (lambda i: (i, c))
    const = lambda i: (0, 0)
    return pl.pallas_call(
        kern,
        grid=(n // tm,),
        in_specs=[pl.BlockSpec((tm, D_MODEL), row(0)),
                  pl.BlockSpec((tm, D_MODEL), row(COL_GA // D_MODEL)),
                  pl.BlockSpec((tm, D_MODEL), row(COL_GB // D_MODEL)),
                  pl.BlockSpec((tm, CONV_WIDTH), row(COL_CB // CONV_WIDTH)),
                  pl.BlockSpec((tm, CONV_WIDTH), row(COL_CC // CONV_WIDTH)),
                  pl.BlockSpec((tm, CONV_WIDTH), row(COL_CU // CONV_WIDTH)),
                  pl.BlockSpec((tm, D_MODEL), row(0)),
                  pl.BlockSpec((1, 6, D_MODEL), lambda i: (i // tiles_per_seq, 0, 0)),
                  pl.BlockSpec((CONV_K, CONV_WIDTH), const),
                  pl.BlockSpec((1, CONV_WIDTH), const),
                  pl.BlockSpec((CONV_WIDTH, D_MODEL), const),
                  pl.BlockSpec((D_MODEL, D_MODEL), const),
                  pl.BlockSpec((D_MODEL, D_MODEL), const),
                  pl.BlockSpec((1, D_MODEL), const),
                  pl.BlockSpec((D_MODEL, 2 * N_EXPERTS), const),
                  pl.BlockSpec((1, N_EXPERTS), const)],
        out_specs=[pl.BlockSpec((tm, D_MODEL), row(0)),
                   pl.BlockSpec((tm, D_MODEL // 2), row(0)),
                   pl.BlockSpec((tm, TOP_K), row(0)),
                   pl.BlockSpec((tm, TOP_K), row(0)),
                   pl.BlockSpec((1, N_EXPERTS), const)],
        out_shape=[jax.ShapeDtypeStruct((n, D_MODEL), F32),
                   jax.ShapeDtypeStruct((n, D_MODEL // 2), jnp.uint32),
                   jax.ShapeDtypeStruct((n, TOP_K), jnp.int32),
                   jax.ShapeDtypeStruct((n, TOP_K), F32),
                   jax.ShapeDtypeStruct((1, N_EXPERTS), F32)],
        scratch_shapes=[pltpu.VMEM((tm + 8, CONV_WIDTH), F32), pltpu.VMEM((8, CONV_WIDTH), F32)],
        compiler_params=pltpu.CompilerParams(dimension_semantics=("arbitrary",),
                                             vmem_limit_bytes=VMEM_LIMIT),
        name="post",
    )(att, proj, proj, proj, proj, proj, x2, mod3, conv_w, conv_b.reshape(1, -1),
      woc_b, woa_b, wmg_b, norm2_g.reshape(1, -1), wr_split, b_router.reshape(1, -1))


def _scan_kernel(idx_ref, pstart_ref, dest_ref, carry, earlier, *, t):
    @pl.when(pl.program_id(0) == 0)
    def _():
        carry[...] = jnp.zeros(carry.shape, F32)
        r = lax.broadcasted_iota(jnp.int32, (t, t), 0)
        c = lax.broadcasted_iota(jnp.int32, (t, t), 1)
        earlier[...] = jnp.where(c < r, 1.0, 0.0).astype(BF16)

    idx = idx_ref[...]
    lane = lax.broadcasted_iota(jnp.int32, (t, N_EXPERTS), 1)
    hits = [lane == idx[:, k:k + 1] for k in range(TOP_K)]
    member = hits[0].astype(F32)
    for k in range(1, TOP_K):
        member = member + hits[k].astype(F32)
    rank = jnp.dot(earlier[...], member.astype(BF16), preferred_element_type=F32) + carry[...]
    dest_e = pstart_ref[...] + rank
    cols = [jnp.sum(jnp.where(hits[k], dest_e, 0.0), axis=-1, keepdims=True) for k in range(TOP_K)]
    dest_ref[...] = jnp.concatenate(cols, axis=1).astype(jnp.int32)
    carry[...] += jnp.sum(member, axis=0, keepdims=True)


def _scan(idx, pstart):
    n = idx.shape[0]
    t = min(T_SCAN, n)
    return pl.pallas_call(
        functools.partial(_scan_kernel, t=t),
        grid=(n // t,),
        in_specs=[pl.BlockSpec((t, TOP_K), lambda i: (i, 0)),
                  pl.BlockSpec((1, N_EXPERTS), lambda i: (0, 0))],
        out_specs=pl.BlockSpec((t, TOP_K), lambda i: (i, 0)),
        out_shape=jax.ShapeDtypeStruct((n, TOP_K), jnp.int32),
        scratch_shapes=[pltpu.VMEM((1, N_EXPERTS), F32), pltpu.VMEM((t, t), BF16)],
        compiler_params=pltpu.CompilerParams(dimension_semantics=("arbitrary",)),
        name="scan",
    )(idx, pstart)


def _sc_mesh():
    return plsc.VectorSubcoreMesh(core_axis_name="core", subcore_axis_name="subcore")


def _sc_worker(mesh):
    return lax.axis_index("core") * mesh.num_subcores + lax.axis_index("subcore")


def _sc_dispatch(dest_t, hp, rows):
    n, width = hp.shape
    mesh = _sc_mesh()
    workers = mesh.num_cores * mesh.num_subcores
    per_worker = n // (SC_WINDOW * workers)
    assert per_worker * SC_WINDOW * workers == n

    @pl.kernel(out_type=jax.ShapeDtypeStruct((rows, width), hp.dtype), mesh=mesh,
               scratch_types=[pltpu.VMEM((SC_WINDOW, width), hp.dtype),
                              pltpu.VMEM((TOP_K, SC_WINDOW), jnp.int32)])
    def scatter(hp_hbm, dest_hbm, xs_hbm, row_buf, idx_buf):
        first = _sc_worker(mesh) * per_worker

        @pl.loop(0, per_worker)
        def _(w):
            t0 = (first + w) * SC_WINDOW
            pltpu.sync_copy(hp_hbm.at[pl.ds(t0, SC_WINDOW)], row_buf)
            pltpu.sync_copy(dest_hbm.at[:, pl.ds(t0, SC_WINDOW)], idx_buf)
            for k in range(TOP_K):
                pltpu.sync_copy(row_buf, xs_hbm.at[idx_buf.at[k]])

    return scatter(hp, dest_t)


def _sc_gather(src_rows, table):
    n_out = src_rows.shape[1]
    width = table.shape[1]
    mesh = _sc_mesh()
    workers = mesh.num_cores * mesh.num_subcores
    per_worker = n_out // (SC_WINDOW * workers)
    assert per_worker * SC_WINDOW * workers == n_out

    @pl.kernel(out_type=jax.ShapeDtypeStruct((n_out, width), table.dtype), mesh=mesh,
               scratch_types=[pltpu.VMEM((SC_WINDOW, width), table.dtype),
                              pltpu.VMEM((1, SC_WINDOW), jnp.int32)])
    def gather(table_hbm, src_hbm, out_hbm, row_buf, idx_buf):
        first = _sc_worker(mesh) * per_worker

        @pl.loop(0, per_worker)
        def _(w):
            r0 = (first + w) * SC_WINDOW
            pltpu.sync_copy(src_hbm.at[:, pl.ds(r0, SC_WINDOW)], idx_buf)
            pltpu.sync_copy(table_hbm.at[idx_buf.at[0]], row_buf)
            pltpu.sync_copy(row_buf, out_hbm.at[pl.ds(r0, SC_WINDOW)])

    return gather(table, src_rows)


def _expert_kernel(be_ref, first_ref, used_ref, valid_ref, xs_ref, wgu_ref, bg_ref, bl_ref, wd_ref,
                   bd_ref, y_ref, wg_s, wl_s, wd_s, tmp_s):
    del be_ref
    i = pl.program_id(0)

    @pl.when(first_ref[i] == 1)
    def _():
        half = DEINT_CHUNK // 2
        for c in range(wgu_ref.shape[2] // DEINT_CHUNK):
            for db in range(D_MODEL // LANES):
                rows = slice(db * LANES, (db + 1) * LANES)
                tmp_s[...] = wgu_ref[0, rows, c * DEINT_CHUNK:(c + 1) * DEINT_CHUNK].T
                feats = slice(c * half, (c + 1) * half)
                wg_s[feats, rows] = tmp_s[pl.ds(0, half, stride=2), :].astype(BF16)
                wl_s[feats, rows] = tmp_s[pl.ds(1, half, stride=2), :].astype(BF16)
        wd_s[...] = wd_ref[0].astype(BF16)

    @pl.when(i < used_ref[0])
    def _():
        xp = xs_ref[...]
        rid = lax.broadcasted_iota(jnp.int32, xp.shape, 0)
        xp = jnp.where(rid < valid_ref[i], xp, jnp.uint32(0))
        xb = jnp.concatenate(_unpack_halves(xp), axis=1).astype(BF16)
        nt = (((1,), (1,)), ((), ()))
        g = lax.dot_general(xb, wg_s[...], nt, preferred_element_type=F32) + bg_ref[0]
        l = lax.dot_general(xb, wl_s[...], nt, preferred_element_type=F32) + bl_ref[0]
        glu = jnp.minimum(g, SWIGLU_LIMIT)
        lin = jnp.clip(l, -SWIGLU_LIMIT, SWIGLU_LIMIT)
        act = glu * _sigmoid(SWIGLU_ALPHA * glu) * (lin + 1.0)
        y = jnp.dot(act.astype(BF16), wd_s[...], preferred_element_type=F32) + bd_ref[0]
        y_ref[...] = _pack_halves(y)

    @pl.when(i >= used_ref[0])
    def _():
        y_ref[...] = jnp.zeros(y_ref.shape, y_ref.dtype)


def _experts(block_e, first, n_used, valid, xs, wgu, bg, bl, wd, bd):
    rows, width = xs.shape
    d_exp = wd.shape[1]
    n_blocks = rows // ROW_BLOCK
    wsel = lambda i, be, fi, nu, va: (be[i], 0, 0)
    rsel = lambda i, be, fi, nu, va: (jnp.minimum(i, nu[0] - 1), 0)
    grid_spec = pltpu.PrefetchScalarGridSpec(
        num_scalar_prefetch=4,
        grid=(n_blocks,),
        in_specs=[pl.BlockSpec((ROW_BLOCK, width), rsel),
                  pl.BlockSpec((1, D_MODEL, 2 * d_exp), wsel),
                  pl.BlockSpec((1, 1, d_exp), wsel),
                  pl.BlockSpec((1, 1, d_exp), wsel),
                  pl.BlockSpec((1, d_exp, D_MODEL), wsel),
                  pl.BlockSpec((1, 1, D_MODEL), wsel)],
        out_specs=pl.BlockSpec((ROW_BLOCK, width), lambda i, be, fi, nu, va: (i, 0)),
        scratch_shapes=[pltpu.VMEM((d_exp, D_MODEL), BF16), pltpu.VMEM((d_exp, D_MODEL), BF16),
                        pltpu.VMEM((d_exp, D_MODEL), BF16), pltpu.VMEM((DEINT_CHUNK, LANES), F32)],
    )
    return pl.pallas_call(
        _expert_kernel,
        grid_spec=grid_spec,
        out_shape=jax.ShapeDtypeStruct((rows, width), xs.dtype),
        compiler_params=pltpu.CompilerParams(dimension_semantics=("arbitrary",),
                                             vmem_limit_bytes=VMEM_LIMIT),
        name="experts",
    )(block_e, first, n_used, valid, xs, wgu, bg, bl, wd, bd)


def _combine_kernel(y0_ref, y1_ref, y2_ref, y3_ref, x1_ref, tw_ref, mod_ref, gf_ref, *rest):
    o_ref = rest[-1]
    tw = tw_ref[...]
    lo = hi = None
    for k, yk_ref in enumerate((y0_ref, y1_ref, y2_ref, y3_ref)):
        y_lo, y_hi = _unpack_halves(yk_ref[...])
        w = tw[:, k:k + 1]
        lo = w * y_lo if lo is None else lo + w * y_lo
        hi = w * y_hi if hi is None else hi + w * y_hi
    moe = jnp.concatenate([lo, hi], axis=1)
    xo = x1_ref[...] + mod_ref[0, 5:6, :] * moe
    o_ref[...] = _rms(xo, gf_ref[...])


def _combine(yk, x1, tw, mod3, normf_g, seq, tile0, prev):
    n = x1.shape[0]
    t = min(TM_POST, seq)
    tiles_per_seq = seq // t
    n_c = yk.shape[0] // TOP_K
    slot = lambda k: pl.BlockSpec((t, yk.shape[1]), lambda i: (k * (n_c // t) + i, 0))
    tok = lambda w: pl.BlockSpec((t, w), lambda i: (tile0 + i, 0))
    in_specs = [slot(0), slot(1), slot(2), slot(3), tok(D_MODEL), tok(TOP_K),
                pl.BlockSpec((1, 6, D_MODEL), lambda i: ((tile0 + i) // tiles_per_seq, 0, 0)),
                pl.BlockSpec((1, D_MODEL), lambda i: (0, 0))]
    args = [yk, yk, yk, yk, x1, tw, mod3, normf_g.reshape(1, -1)]
    aliases = {}
    if prev is not None:
        aliases = {len(args): 0}
        in_specs.append(pl.BlockSpec(memory_space=pl.ANY))
        args.append(prev)
    return pl.pallas_call(
        _combine_kernel,
        grid=(n_c // t,),
        in_specs=in_specs,
        out_specs=tok(D_MODEL),
        out_shape=jax.ShapeDtypeStruct((n, D_MODEL), F32),
        input_output_aliases=aliases,
        compiler_params=pltpu.CompilerParams(dimension_semantics=("arbitrary",)),
        name="combine",
    )(*args)


def _in_proj_weights(w_in):
    c0 = 3 * CONV_WIDTH
    q, k, v = (w_in[:, c0 + s * D_MODEL:c0 + (s + 1) * D_MODEL] for s in range(3))
    w_nat = jnp.concatenate([k, w_in[:, c0 + 3 * D_MODEL:], w_in[:, :c0]], axis=1).astype(BF16)
    w_tr = jnp.concatenate([q, v], axis=1).T.astype(BF16)
    return w_nat, w_tr


def kernel(x, c, w_ada, b_ada, norm1_g, norm2_g, w_in, conv_w, conv_b, w_out_conv, lambda_q1, lambda_k1, lambda_q2, lambda_k2, subln_g, w_o_attn, w_merge, w_router, b_router, w_gate_up, b_gate_up, w_down, b_down, rel_bias_table, normf_g):
    bsz, seq, _ = x.shape
    n = bsz * seq
    assert w_ada.shape[0] == 1, "single layer"
    assert seq % (2 * min(TQ, seq // 2)) == 0 and n % TM_POST == 0 and (n * TOP_K) % ROW_BLOCK == 0

    mod, lam = _ada(c, w_ada[0], b_ada[0], lambda_q1[0], lambda_k1[0], lambda_q2[0], lambda_k2[0])
    mod3 = mod.reshape(bsz, 6, D_MODEL)

    proj, proj_t = _inproj(x, mod3, norm1_g[0], *_in_proj_weights(w_in[0]))
    att = _attention(proj, proj_t, lam[0, :1], _bias_tiles(rel_bias_table, min(TQ, seq // 2)), subln_g[0])

    wr = w_router[0]
    wr_hi = wr.astype(BF16)
    wr_lo = (wr - wr_hi.astype(F32)).astype(BF16)
    x1, hp, idx, tw, counts = _post(
        att.reshape(n, -1), proj.reshape(n, -1), x.reshape(n, -1), mod3, conv_w[0], conv_b[0],
        w_out_conv[0].astype(BF16), w_o_attn[0].astype(BF16), w_merge[0].astype(BF16),
        norm2_g[0], jnp.concatenate([wr_hi, wr_lo], axis=1), b_router[0], seq)

    cnt = counts[0].astype(jnp.int32)
    padded = (cnt + ROW_BLOCK - 1) // ROW_BLOCK * ROW_BLOCK
    pend = jnp.cumsum(padded)
    pstart = pend - padded
    n_blocks = n * TOP_K // ROW_BLOCK + N_EXPERTS
    rows = n_blocks * ROW_BLOCK
    block_start = jnp.arange(n_blocks, dtype=jnp.int32) * ROW_BLOCK
    block_e = jnp.minimum(jnp.sum(pend[None, :] <= block_start[:, None], axis=1),
                          N_EXPERTS - 1).astype(jnp.int32)

    dest_t = _scan(idx, pstart.astype(F32).reshape(1, -1)).T
    xs = _sc_dispatch(dest_t, hp, rows)

    n_used = (pend[-1:] // ROW_BLOCK).astype(jnp.int32)
    changed = jnp.concatenate([jnp.ones((1,), jnp.bool_), block_e[1:] != block_e[:-1]])
    first = (changed & (jnp.arange(n_blocks) < n_used[0])).astype(jnp.int32)
    valid = jnp.clip((pstart + cnt)[block_e] - block_start, 0, ROW_BLOCK).astype(jnp.int32)
    bgu = b_gate_up[0]
    y = _experts(block_e, first, n_used, valid, xs, w_gate_up[0], bgu[:, None, 0::2],
                 bgu[:, None, 1::2], w_down[0], b_down[0][:, None, :])

    n_c = n // COMBINE_CHUNKS
    out = None
    for ch in range(COMBINE_CHUNKS):
        src = dest_t[:, ch * n_c:(ch + 1) * n_c].reshape(1, -1)
        out = _combine(_sc_gather(src, y), x1, tw, mod3, normf_g, seq, ch * n_c // min(TM_POST, seq), out)
    return out.reshape(bsz, seq, D_MODEL)
```
